```python
import math
import jax, jax.numpy as jnp
from jax import lax
import numpy as np

D_MODEL = 2048
BATCH = 2
SEQ = 8192
DEPTH = 2

N_MIXERS = 2
N_HEADS = 16
HEAD_DIM = D_MODEL // N_HEADS
DILATED_GROUPS = ((128, 1), (512, 4), (2048, 16))
N_GROUPS = len(DILATED_GROUPS)
D_FF = 5632
CONV_W = 3
NORM_EPS = 1e-5
ALIBI_MAX = 8.0
NEG_INF = -1e30

kernel_name = "hybrid_shortconv_dilated_alibi_encoder"


def alibi_slopes(n_heads):
    return jnp.asarray(2.0 ** (-ALIBI_MAX * np.arange(1, n_heads + 1) / n_heads), dtype=jnp.float32)


def rmsnorm(x, g):
    xf = x.astype(jnp.float32)
    y = xf * lax.rsqrt(jnp.mean(xf * xf, axis=-1, keepdims=True) + NORM_EPS)
    return (y * g.astype(jnp.float32)).astype(x.dtype)


def dwconv3(u, w, b):
    up = jnp.pad(u, ((0, 0), (1, 1), (0, 0)))
    return up[:, :-2] * w[0] + up[:, 1:-1] * w[1] + up[:, 2:] * w[2] + b


def short_conv_mixer(h, w_in, conv_w, conv_b, w_out):
    z = h @ w_in
    u, gate_b, gate_c = jnp.split(z, 3, axis=-1)
    y = gate_b * dwconv3(gate_c * u, conv_w, conv_b)
    return y @ w_out


def to_strided(x, d):
    B, S = x.shape[:2]
    rest = x.shape[2:]
    y = x.reshape((B, S // d, d) + rest)
    y = jnp.swapaxes(y, 1, 2)
    return y.reshape((B * d, S // d) + rest)


def from_strided(y, d, B):
    N, L = y.shape[:2]
    rest = y.shape[2:]
    z = y.reshape((B, d, L) + rest)
    z = jnp.swapaxes(z, 1, 2)
    return z.reshape((B, L * d) + rest)


def banded_attention(q, k, v, slopes, dil, half):
    N, L, H, Dh = q.shape
    blk = half
    nb = -(-L // blk)
    Lp = nb * blk
    qb = jnp.pad(q, ((0, 0), (0, Lp - L), (0, 0), (0, 0))).reshape(N, nb, blk, H, Dh)
    kp = jnp.pad(k, ((0, 0), (blk, Lp - L + blk), (0, 0), (0, 0))).reshape(N, nb + 2, blk, H, Dh)
    vp = jnp.pad(v, ((0, 0), (blk, Lp - L + blk), (0, 0), (0, 0))).reshape(N, nb + 2, blk, H, Dh)
    kw = jnp.concatenate([kp[:, :-2], kp[:, 1:-1], kp[:, 2:]], axis=2)
    vw = jnp.concatenate([vp[:, :-2], vp[:, 1:-1], vp[:, 2:]], axis=2)
    qi = jnp.arange(Lp).reshape(nb, blk)
    kj = jnp.arange(nb)[:, None] * blk + jnp.arange(3 * blk)[None, :] - blk
    delta = jnp.abs(kj[:, None, :] - qi[:, :, None])
    valid = (delta <= half) & ((kj >= 0) & (kj < L))[:, None, :]
    s = jnp.einsum('nbqhd,nbkhd->nbhqk', qb, kw).astype(jnp.float32) * (Dh ** -0.5)
    bias = -slopes[None, :, None, None] * (dil * delta).astype(jnp.float32)[:, None, :, :]
    s = jnp.where(valid[:, None, :, :][None], s + bias[None], NEG_INF)
    m = jnp.max(s, axis=-1, keepdims=True)
    p = jnp.exp(s - m)
    den = jnp.sum(p, axis=-1)
    o = jnp.einsum('nbhqk,nbkhd->nbqhd', p, vw.astype(jnp.float32))
    den_t = jnp.swapaxes(den, 2, 3)
    o = o / den_t[..., None]
    lse = jnp.swapaxes(m[..., 0], 2, 3) + jnp.log(den_t)
    return o.reshape(N, Lp, H, Dh)[:, :L], lse.reshape(N, Lp, H)[:, :L]


def dilated_attention_mixer(h, w_qkv, w_out):
    B, S, D = h.shape
    qkv = (h @ w_qkv).reshape(B, S, N_GROUPS, 3, N_HEADS, HEAD_DIM)
    slopes = alibi_slopes(N_HEADS)
    outs, lses = [], []
    for g, (window, dil) in enumerate(DILATED_GROUPS):
        half = (window // 2) // dil
        q = to_strided(qkv[:, :, g, 0], dil)
        k = to_strided(qkv[:, :, g, 1], dil)
        v = to_strided(qkv[:, :, g, 2], dil)
        o, lse = banded_attention(q, k, v, slopes, dil, half)
        outs.append(from_strided(o, dil, B))
        lses.append(from_strided(lse, dil, B))
    wts = jax.nn.softmax(jnp.stack(lses, axis=0), axis=0)
    o = jnp.sum(wts[..., None] * jnp.stack(outs, axis=0), axis=0)
    return o.reshape(B, S, D).astype(h.dtype) @ w_out


def conv_ffn(h, w_up, conv_w, conv_b, w_down):
    u = dwconv3(h @ w_up, conv_w, conv_b)
    a, b = jnp.split(u, 2, axis=-1)
    return (jax.nn.silu(a) * b) @ w_down


def setup_inputs(seed: int = 0) -> dict:
    key = jax.random.key(seed)
    ks = jax.random.split(key, 16)
    D, F = D_MODEL, D_FF
    n_a = (DEPTH + N_MIXERS - 1) // N_MIXERS
    n_b = DEPTH // N_MIXERS
    nrm = lambda k, shape, s: jax.random.normal(k, shape, jnp.float32) * s
    return {
        "x": nrm(ks[0], (BATCH, SEQ, D), 1.0),
        "mix_norm_g": 1.0 + nrm(ks[1], (DEPTH, D), 0.02),
        "ffn_norm_g": 1.0 + nrm(ks[2], (DEPTH, D), 0.02),
        "final_norm_g": 1.0 + nrm(ks[3], (D,), 0.02),
        "sc_w_in": nrm(ks[4], (n_a, D, 3 * D), D ** -0.5),
        "sc_conv_w": nrm(ks[5], (n_a, CONV_W, D), CONV_W ** -0.5),
        "sc_conv_b": nrm(ks[6], (n_a, D), 0.01),
        "sc_w_out": nrm(ks[7], (n_a, D, D), D ** -0.5),
        "attn_w_qkv": nrm(ks[8], (n_b, D, N_GROUPS * 3 * N_HEADS * HEAD_DIM), D ** -0.5),
        "attn_w_out": nrm(ks[9], (n_b, N_HEADS * HEAD_DIM, D), (N_HEADS * HEAD_DIM) ** -0.5),
        "ffn_w_up": nrm(ks[10], (DEPTH, D, 2 * F), D ** -0.5),
        "ffn_conv_w": nrm(ks[11], (DEPTH, CONV_W, 2 * F), CONV_W ** -0.5),
        "ffn_conv_b": nrm(ks[12], (DEPTH, 2 * F), 0.01),
        "ffn_w_down": nrm(ks[13], (DEPTH, F, D), F ** -0.5),
    }


def reference(x, mix_norm_g, ffn_norm_g, final_norm_g, sc_w_in, sc_conv_w, sc_conv_b, sc_w_out,
              attn_w_qkv, attn_w_out, ffn_w_up, ffn_conv_w, ffn_conv_b, ffn_w_down):
    for i in range(DEPTH):
        h = rmsnorm(x, mix_norm_g[i])
        j = i // N_MIXERS
        if i % N_MIXERS == 0:
            x = x + short_conv_mixer(h, sc_w_in[j], sc_conv_w[j], sc_conv_b[j], sc_w_out[j])
        else:
            x = x + dilated_attention_mixer(h, attn_w_qkv[j], attn_w_out[j])
        h = rmsnorm(x, ffn_norm_g[i])
        x = x + conv_ffn(h, ffn_w_up[i], ffn_conv_w[i], ffn_conv_b[i], ffn_w_down[i])
    return rmsnorm(x, final_norm_g)
```

```python
import functools

import numpy as np
import jax
import jax.numpy as jnp
from jax import lax
from jax.experimental import pallas as pl
from jax.experimental.pallas import tpu as pltpu

N_HEADS = 16
DILATED_GROUPS = ((128, 1), (512, 4), (2048, 16))
NORM_EPS = 1e-5
ALIBI_MAX = 8.0
NEG_INF = -1e30

LANES = 128
BF16_ROWS = 16
VMEM_LIMIT = 56 * 1024 * 1024

F32 = jnp.float32
BF16 = jnp.bfloat16


def _params(semantics):
    return pltpu.CompilerParams(dimension_semantics=semantics,
                                vmem_limit_bytes=VMEM_LIMIT)


def _norm_matmul_kernel(x_ref, g_ref, w_ref, o_ref, h_ref):
    @pl.when(pl.program_id(1) == 0)
    def _():
        x = x_ref[...]
        ms = jnp.mean(x * x, axis=-1, keepdims=True)
        h_ref[...] = (x * lax.rsqrt(ms + NORM_EPS) * g_ref[...]).astype(h_ref.dtype)

    o_ref[...] = jnp.dot(h_ref[...], w_ref[...],
                         preferred_element_type=F32).astype(o_ref.dtype)


def _norm_matmul(x2d, gain, w, *, seq_len, dil, col0, n_cols, tm, tn):
    T, D = x2d.shape
    L = seq_len // dil
    assert L % tm == 0 and n_cols % tn == 0 and col0 % tn == 0
    n_l = L // tm
    xv = x2d.reshape(T // dil, dil * D)

    def x_map(m, j):
        n, it = m // n_l, m % n_l
        return ((n // dil) * n_l + it, n % dil)

    return pl.pallas_call(
        _norm_matmul_kernel,
        grid=(T // tm, n_cols // tn),
        in_specs=[
            pl.BlockSpec((tm, D), x_map),
            pl.BlockSpec((1, D), lambda m, j: (0, 0)),
            pl.BlockSpec((D, tn), lambda m, j: (0, col0 // tn + j)),
        ],
        out_specs=pl.BlockSpec((tm, tn), lambda m, j: (m, j)),
        out_shape=jax.ShapeDtypeStruct((T, n_cols), BF16),
        scratch_shapes=[pltpu.VMEM((tm, D), BF16)],
        compiler_params=_params(("parallel", "arbitrary")),
        name="norm_matmul",
    )(xv, gain.reshape(1, D), w)


def _halo_maps(tm, n_rows):
    per = tm // BF16_ROWS
    last = n_rows // BF16_ROWS - 1
    prev_map = lambda i: jnp.maximum(i * per - 1, 0)
    next_map = lambda i: jnp.minimum((i + 1) * per, last)
    return prev_map, next_map


def _dwconv3(v, prev_row, next_row, w, b):
    tm = v.shape[0]
    row = lax.broadcasted_iota(jnp.int32, v.shape, 0)
    v_prev = jnp.where(row == 0, prev_row, pltpu.roll(v, 1, 0))
    v_next = jnp.where(row == tm - 1, next_row, pltpu.roll(v, tm - 1, 0))
    return v_prev * w[0:1, :] + v * w[1:2, :] + v_next * w[2:3, :] + b


def _shortconv_out_kernel(z_ref, zp_ref, zn_ref, x_ref, cw_ref, cb_ref, w_ref, o_ref,
                          *, tiles_per_seq):
    i = pl.program_id(0)
    D = x_ref.shape[1]
    first = (i % tiles_per_seq) == 0
    last = ((i + 1) % tiles_per_seq) == 0

    def gated(ref, rows):
        u = ref[rows, 0:D].astype(F32)
        gc = ref[rows, 2 * D:3 * D].astype(F32)
        return gc * u

    v = gated(z_ref, slice(None))
    vp = gated(zp_ref, slice(None))[BF16_ROWS - 1:BF16_ROWS, :]
    vn = gated(zn_ref, slice(None))[0:1, :]
    vp = jnp.where(first, 0.0, vp)
    vn = jnp.where(last, 0.0, vn)
    y = z_ref[:, D:2 * D].astype(F32) * _dwconv3(v, vp, vn, cw_ref[...], cb_ref[...])
    o_ref[...] = x_ref[...] + jnp.dot(y.astype(BF16), w_ref[...],
                                      preferred_element_type=F32)


def _shortconv_out(z, x2d, conv_w, conv_b, w_out, *, seq_len, tm):
    T, D = x2d.shape
    assert seq_len % tm == 0
    prev_map, next_map = _halo_maps(tm, T)
    kern = functools.partial(_shortconv_out_kernel, tiles_per_seq=seq_len // tm)
    return pl.pallas_call(
        kern,
        grid=(T // tm,),
        in_specs=[
            pl.BlockSpec((tm, 3 * D), lambda i: (i, 0)),
            pl.BlockSpec((BF16_ROWS, 3 * D), lambda i: (prev_map(i), 0)),
            pl.BlockSpec((BF16_ROWS, 3 * D), lambda i: (next_map(i), 0)),
            pl.BlockSpec((tm, D), lambda i: (i, 0)),
            pl.BlockSpec((3, D), lambda i: (0, 0)),
            pl.BlockSpec((1, D), lambda i: (0, 0)),
            pl.BlockSpec((D, D), lambda i: (0, 0)),
        ],
        out_specs=pl.BlockSpec((tm, D), lambda i: (i, 0)),
        out_shape=jax.ShapeDtypeStruct((T, D), F32),
        compiler_params=_params(("parallel",)),
        name="shortconv_out",
    )(z, z, z, x2d, conv_w, conv_b.reshape(1, D), w_out)


def _ffn_down_kernel(ua_ref, uap_ref, uan_ref, ub_ref, ubp_ref, ubn_ref,
                     cwa_ref, cwb_ref, cba_ref, cbb_ref, w_ref, x_ref, g_ref,
                     o_ref, acc_ref, *, tiles_per_seq, final_norm):
    i, k = pl.program_id(0), pl.program_id(1)
    first = (i % tiles_per_seq) == 0
    last = ((i + 1) % tiles_per_seq) == 0

    def conv(u_ref, up_ref, un_ref, cw_ref, cb_ref):
        up = jnp.where(first, 0.0, up_ref[BF16_ROWS - 1:BF16_ROWS, :].astype(F32))
        un = jnp.where(last, 0.0, un_ref[0:1, :].astype(F32))
        return _dwconv3(u_ref[...].astype(F32), up, un, cw_ref[...], cb_ref[...])

    a = conv(ua_ref, uap_ref, uan_ref, cwa_ref, cba_ref)
    b = conv(ub_ref, ubp_ref, ubn_ref, cwb_ref, cbb_ref)
    gated = (a / (1.0 + jnp.exp(-a))) * b
    part = jnp.dot(gated.astype(BF16), w_ref[...], preferred_element_type=F32)

    @pl.when(k == 0)
    def _():
        acc_ref[...] = part

    @pl.when(k > 0)
    def _():
        acc_ref[...] += part

    @pl.when(k == pl.num_programs(1) - 1)
    def _():
        y = x_ref[...] + acc_ref[...]
        if final_norm:
            ms = jnp.mean(y * y, axis=-1, keepdims=True)
            y = y * lax.rsqrt(ms + NORM_EPS) * g_ref[...]
        o_ref[...] = y


def _ffn_down(u, x2d, conv_w, conv_b, w_down, final_gain, *, seq_len, tm, tk):
    T, D = x2d.shape
    F = w_down.shape[0]
    assert seq_len % tm == 0 and F % tk == 0
    nk = F // tk
    prev_map, next_map = _halo_maps(tm, T)
    final_norm = final_gain is not None
    gain = final_gain if final_norm else jnp.ones((D,), F32)
    kern = functools.partial(_ffn_down_kernel, tiles_per_seq=seq_len // tm,
                             final_norm=final_norm)
    cb = conv_b.reshape(1, 2 * F)

    def tile(off):
        return pl.BlockSpec((tm, tk), lambda i, k: (i, off + k))

    def halo(row_map, off):
        return pl.BlockSpec((BF16_ROWS, tk), lambda i, k: (row_map(i), off + k))

    def chan(rows, off):
        return pl.BlockSpec((rows, tk), lambda i, k: (0, off + k))

    return pl.pallas_call(
        kern,
        grid=(T // tm, nk),
        in_specs=[
            tile(0), halo(prev_map, 0), halo(next_map, 0),
            tile(nk), halo(prev_map, nk), halo(next_map, nk),
            chan(3, 0), chan(3, nk), chan(1, 0), chan(1, nk),
            pl.BlockSpec((tk, D), lambda i, k: (k, 0)),
            pl.BlockSpec((tm, D), lambda i, k: (i, 0)),
            pl.BlockSpec((1, D), lambda i, k: (0, 0)),
        ],
        out_specs=pl.BlockSpec((tm, D), lambda i, k: (i, 0)),
        out_shape=jax.ShapeDtypeStruct((T, D), F32),
        scratch_shapes=[pltpu.VMEM((tm, D), F32)],
        compiler_params=_params(("parallel", "arbitrary")),
        name="ffn_down",
    )(u, u, u, u, u, u, conv_w, conv_w, cb, cb, w_down, x2d, gain.reshape(1, D))


def _alibi_slope(h):
    return float(np.float32(2.0 ** (-ALIBI_MAX * (h + 1) / N_HEADS)))


def _banded_attention_kernel(q_ref, k_ref, kp_ref, kn_ref, v_ref, vp_ref, vn_ref,
                             o_ref, lse_ref, kbuf, vbuf, *, dil, seq, half, qs):
    lq = q_ref.shape[0]
    dh = q_ref.shape[1] // N_HEADS
    qi = pl.program_id(1)
    for buf, prev, main, nxt in ((kbuf, kp_ref, k_ref, kn_ref), (vbuf, vp_ref, v_ref, vn_ref)):
        buf[0:half, :] = prev[...]
        buf[half:half + lq, :] = main[...]
        buf[half + lq:, :] = nxt[...]

    win = qs + 2 * half
    ii = lax.broadcasted_iota(jnp.int32, (qs, win), 0)
    jj = lax.broadcasted_iota(jnp.int32, (qs, win), 1)
    delta = jnp.abs(jj - half - ii)
    band = delta <= half
    dist = (dil * delta).astype(F32)
    lane = lax.broadcasted_iota(jnp.int32, (qs, LANES), 1)
    scale = dh ** -0.5

    for sb in range(lq // qs):
        key_pos = qi * lq + (sb * qs - half) + jj
        valid = band & (key_pos >= 0) & (key_pos < seq)
        lse_tile = jnp.zeros((qs, LANES), F32)
        for h in range(N_HEADS):
            cols = slice(h * dh, (h + 1) * dh)
            q = q_ref[sb * qs:(sb + 1) * qs, cols]
            kw = kbuf[sb * qs:sb * qs + win, cols]
            vw = vbuf[sb * qs:sb * qs + win, cols]
            s = lax.dot_general(q, kw, (((1,), (1,)), ((), ())),
                                preferred_element_type=F32) * scale
            s = jnp.where(valid, s + (-_alibi_slope(h)) * dist, NEG_INF)
            m = jnp.max(s, axis=-1, keepdims=True)
            p = jnp.exp(s - m)
            den = jnp.sum(p, axis=-1, keepdims=True)
            o = jnp.dot(p.astype(BF16), vw, preferred_element_type=F32)
            o_ref[sb * qs:(sb + 1) * qs, cols] = (o / den).astype(o_ref.dtype)
            lse_tile = jnp.where(lane == h, m + jnp.log(den), lse_tile)
        lse_ref[sb * qs:(sb + 1) * qs, :] = lse_tile


def _banded_attention(qkv, *, seq_len, window, dil, lq, qs):
    T, D3 = qkv.shape
    D = D3 // 3
    L = seq_len // dil
    half = (window // 2) // dil
    assert L % lq == 0 and lq % qs == 0 and lq % half == 0 and half % BF16_ROWS == 0
    n_q = L // lq
    per = lq // half
    last = T // half - 1

    def main(c):
        return pl.BlockSpec((lq, D), lambda n, i: (n * n_q + i, c))

    def prev(c):
        return pl.BlockSpec((half, D), lambda n, i: (jnp.maximum((n * n_q + i) * per - 1, 0), c))

    def nxt(c):
        return pl.BlockSpec((half, D), lambda n, i: (jnp.minimum((n * n_q + i + 1) * per, last), c))

    kern = functools.partial(_banded_attention_kernel, dil=dil, seq=L, half=half, qs=qs)
    return pl.pallas_call(
        kern,
        grid=(T // L, n_q),
        in_specs=[main(0), main(1), prev(1), nxt(1), main(2), prev(2), nxt(2)],
        out_specs=[pl.BlockSpec((lq, D), lambda n, i: (n * n_q + i, 0)),
                   pl.BlockSpec((lq, LANES), lambda n, i: (n * n_q + i, 0))],
        out_shape=[jax.ShapeDtypeStruct((T, D), BF16),
                   jax.ShapeDtypeStruct((T, LANES), F32)],
        scratch_shapes=[pltpu.VMEM((lq + 2 * half, D), BF16),
                        pltpu.VMEM((lq + 2 * half, D), BF16)],
        compiler_params=_params(("parallel", "parallel")),
        name="banded_attention",
    )(qkv, qkv, qkv, qkv, qkv, qkv, qkv)


def _combine_out_kernel(*refs, n_groups):
    o_refs = refs[:n_groups]
    l_refs = refs[n_groups:2 * n_groups]
    x_ref, w_ref, out_ref, mix_ref = refs[2 * n_groups:]
    dh = x_ref.shape[1] // N_HEADS
    lses = [r[...] for r in l_refs]
    m = functools.reduce(jnp.maximum, lses)
    es = [jnp.exp(l - m) for l in lses]
    total = functools.reduce(lambda a, b: a + b, es)
    wts = [e / total for e in es]
    for h in range(N_HEADS):
        cols = slice(h * dh, (h + 1) * dh)
        mixed = functools.reduce(
            lambda a, b: a + b,
            [wt[:, h:h + 1] * o[:, cols].astype(F32) for wt, o in zip(wts, o_refs)])
        mix_ref[:, cols] = mixed.astype(mix_ref.dtype)
    out_ref[...] = x_ref[...] + jnp.dot(mix_ref[...], w_ref[...],
                                        preferred_element_type=F32)


def _combine_out(outs, lses, dils, x2d, w_out, *, seq_len, tm):
    T, D = x2d.shape
    dmax = max(dils)
    lmax = seq_len // dmax
    assert lmax % tm == 0 and all(dmax % d == 0 for d in dils)
    per = lmax // tm

    def group_specs(d, width):
        def imap(r, it):
            b, i = it // per, it % per
            return ((b * d + r % d) * per + i, r // d)
        return pl.BlockSpec((tm, width), imap)

    views = lambda a, d, width: a.reshape(T * d // dmax, (dmax // d) * width)
    token_spec = pl.BlockSpec((tm, D), lambda r, it: (it, r))
    n = len(dils)
    return pl.pallas_call(
        functools.partial(_combine_out_kernel, n_groups=n),
        grid=(dmax, T // dmax // tm),
        in_specs=([group_specs(d, D) for d in dils]
                  + [group_specs(d, LANES) for d in dils]
                  + [token_spec, pl.BlockSpec((D, D), lambda r, it: (0, 0))]),
        out_specs=token_spec,
        out_shape=jax.ShapeDtypeStruct((T // dmax, dmax * D), F32),
        scratch_shapes=[pltpu.VMEM((tm, D), BF16)],
        compiler_params=_params(("parallel", "parallel")),
        name="combine_out",
    )(*[views(o, d, D) for o, d in zip(outs, dils)],
      *[views(l, d, LANES) for l, d in zip(lses, dils)],
      x2d.reshape(T // dmax, dmax * D), w_out).reshape(T, D)


def kernel(x, mix_norm_g, ffn_norm_g, final_norm_g, sc_w_in, sc_conv_w, sc_conv_b, sc_w_out,
           attn_w_qkv, attn_w_out, ffn_w_up, ffn_conv_w, ffn_conv_b, ffn_w_down):
    B, S, D = x.shape
    T = B * S
    depth = mix_norm_g.shape[0]
    n_mixers = 2
    xf = x.reshape(T, D)
    for i in range(depth):
        j = i // n_mixers
        if i % n_mixers == 0:
            z = _norm_matmul(xf, mix_norm_g[i], sc_w_in[j].astype(BF16), seq_len=S, dil=1,
                             col0=0, n_cols=3 * D, tm=1024, tn=1024)
            xf = _shortconv_out(z, xf, sc_conv_w[j], sc_conv_b[j], sc_w_out[j].astype(BF16),
                                seq_len=S, tm=256)
        else:
            w_qkv = attn_w_qkv[j].astype(BF16)
            outs, lses, dils = [], [], []
            for g, (window, dil) in enumerate(DILATED_GROUPS):
                qkv = _norm_matmul(xf, mix_norm_g[i], w_qkv, seq_len=S, dil=dil,
                                   col0=g * 3 * D, n_cols=3 * D,
                                   tm=min(1024, S // dil), tn=1024)
                o, lse = _banded_attention(qkv, seq_len=S, window=window, dil=dil,
                                           lq=256, qs=128)
                outs.append(o)
                lses.append(lse)
                dils.append(dil)
            xf = _combine_out(outs, lses, dils, xf, attn_w_out[j].astype(BF16),
                              seq_len=S, tm=256)
        u = _norm_matmul(xf, ffn_norm_g[i], ffn_w_up[i].astype(BF16), seq_len=S, dil=1,
                         col0=0, n_cols=ffn_w_up.shape[2], tm=1024, tn=1024)
        xf = _ffn_down(u, xf, ffn_conv_w[i], ffn_conv_b[i], ffn_w_down[i].astype(BF16),
                       final_norm_g if i == depth - 1 else None,
                       seq_len=S, tm=512, tk=512)
    return xf.reshape(B, S, D)
```

```python
import functools

import numpy as np
import jax
import jax.numpy as jnp
from jax import lax
from jax.experimental import pallas as pl
from jax.experimental.pallas import tpu as pltpu

N_HEADS = 16
DILATED_GROUPS = ((128, 1), (512, 4), (2048, 16))
NORM_EPS = 1e-5
ALIBI_MAX = 8.0
NEG_INF = -1e30

LANES = 128
BF16_ROWS = 16
MXU_DIM = 256
VMEM_LIMIT = 56 * 1024 * 1024
ATTN_TOKENS = 1024
ATTN_HEADS = 4

F32 = jnp.float32
BF16 = jnp.bfloat16


def _params(semantics):
    return pltpu.CompilerParams(dimension_semantics=semantics,
                                vmem_limit_bytes=VMEM_LIMIT)


def _rmsnorm(x, gain):
    ms = jnp.mean(x * x, axis=-1, keepdims=True)
    return x * lax.rsqrt(ms + NORM_EPS) * gain


def _log2(n):
    assert n & (n - 1) == 0
    return n.bit_length() - 1


def _norm_matmul_kernel(x_ref, g_ref, w_ref, o_ref, h_ref, *, dil):
    tm = x_ref.shape[0]
    per = tm // dil

    @pl.when(pl.program_id(1) == 0)
    def _():
        h = _rmsnorm(x_ref[...], g_ref[...]).astype(BF16)
        if dil == 1:
            h_ref[...] = h
        else:
            n = MXU_DIM // dil
            row = lax.broadcasted_iota(jnp.int32, (MXU_DIM, MXU_DIM), 0)
            col = lax.broadcasted_iota(jnp.int32, (MXU_DIM, MXU_DIM), 1)
            src = (row & (n - 1)) * dil + lax.shift_right_logical(row, _log2(n))
            perm = jnp.where(col == src, 1.0, 0.0).astype(BF16)
            for c in range(tm // MXU_DIM):
                hc = jnp.dot(perm, h[c * MXU_DIM:(c + 1) * MXU_DIM, :],
                             preferred_element_type=F32).astype(BF16)
                for r in range(dil):
                    h_ref[r * per + c * n:r * per + (c + 1) * n, :] = hc[r * n:(r + 1) * n, :]

    res = jnp.dot(h_ref[...], w_ref[...], preferred_element_type=F32).astype(o_ref.dtype)
    for r in range(dil):
        o_ref[r] = res[r * per:(r + 1) * per, :]


def _norm_matmul(x2d, gain, w, *, batch, dil, col0, n_cols, tm, tn):
    T, D = x2d.shape
    S = T // batch
    assert S % tm == 0 and n_cols % tn == 0 and col0 % tn == 0
    assert tm % MXU_DIM == 0 and (MXU_DIM // dil) % BF16_ROWS == 0
    tiles = S // tm
    return pl.pallas_call(
        functools.partial(_norm_matmul_kernel, dil=dil),
        grid=(T // tm, n_cols // tn),
        in_specs=[
            pl.BlockSpec((tm, D), lambda m, j: (m, 0)),
            pl.BlockSpec((1, D), lambda m, j: (0, 0)),
            pl.BlockSpec((D, tn), lambda m, j: (0, col0 // tn + j)),
        ],
        out_specs=pl.BlockSpec((None, dil, tm // dil, tn),
                               lambda m, j: (m // tiles, 0, m % tiles, j)),
        out_shape=jax.ShapeDtypeStruct((batch, dil, S // dil, n_cols), BF16),
        scratch_shapes=[pltpu.VMEM((tm, D), BF16)],
        compiler_params=_params(("parallel", "arbitrary")),
        name="norm_matmul",
    )(x2d, gain.reshape(1, D), w)


def _halo_maps(tm, n_rows, rows):
    per = tm // rows
    last = n_rows // rows - 1
    prev_map = lambda i: jnp.maximum(i * per - 1, 0)
    next_map = lambda i: jnp.minimum((i + 1) * per, last)
    return prev_map, next_map


def _dwconv3(v, prev_row, next_row, w, b):
    tm = v.shape[0]
    row = lax.broadcasted_iota(jnp.int32, v.shape, 0)
    v_prev = jnp.where(row == 0, prev_row, pltpu.roll(v, 1, 0))
    v_next = jnp.where(row == tm - 1, next_row, pltpu.roll(v, tm - 1, 0))
    return v_prev * w[0:1, :] + v * w[1:2, :] + v_next * w[2:3, :] + b


def _shortconv_out_kernel(z_ref, zp_ref, zn_ref, x_ref, cw_ref, cb_ref, w_ref, o_ref,
                          *, tiles_per_seq):
    i = pl.program_id(0)
    D = x_ref.shape[1]
    first = (i % tiles_per_seq) == 0
    last = ((i + 1) % tiles_per_seq) == 0

    def gated(ref):
        return ref[:, 2 * D:3 * D].astype(F32) * ref[:, 0:D].astype(F32)

    v = gated(z_ref)
    vp = jnp.where(first, 0.0, gated(zp_ref)[BF16_ROWS - 1:BF16_ROWS, :])
    vn = jnp.where(last, 0.0, gated(zn_ref)[0:1, :])
    y = z_ref[:, D:2 * D].astype(F32) * _dwconv3(v, vp, vn, cw_ref[...], cb_ref[...])
    o_ref[...] = x_ref[...] + jnp.dot(y.astype(BF16), w_ref[...],
                                      preferred_element_type=F32)


def _shortconv_out(z, x2d, conv_w, conv_b, w_out, *, seq_len, tm):
    T, D = x2d.shape
    assert seq_len % tm == 0
    prev_map, next_map = _halo_maps(tm, T, BF16_ROWS)
    kern = functools.partial(_shortconv_out_kernel, tiles_per_seq=seq_len // tm)
    return pl.pallas_call(
        kern,
        grid=(T // tm,),
        in_specs=[
            pl.BlockSpec((tm, 3 * D), lambda i: (i, 0)),
            pl.BlockSpec((BF16_ROWS, 3 * D), lambda i: (prev_map(i), 0)),
            pl.BlockSpec((BF16_ROWS, 3 * D), lambda i: (next_map(i), 0)),
            pl.BlockSpec((tm, D), lambda i: (i, 0)),
            pl.BlockSpec((3, D), lambda i: (0, 0)),
            pl.BlockSpec((1, D), lambda i: (0, 0)),
            pl.BlockSpec((D, D), lambda i: (0, 0)),
        ],
        out_specs=pl.BlockSpec((tm, D), lambda i: (i, 0)),
        out_shape=jax.ShapeDtypeStruct((T, D), F32),
        compiler_params=_params(("parallel",)),
        name="shortconv_out",
    )(z, z, z, x2d, conv_w, conv_b.reshape(1, D), w_out)


def _norm_up_gate_kernel(x_ref, xp_ref, xn_ref, g_ref, wa_ref, wb_ref, cwa_ref, cwb_ref,
                         cba_ref, cbb_ref, o_ref, h_ref, ra_ref, rb_ref, *, tiles_per_seq):
    i = pl.program_id(0)
    tm = x_ref.shape[0]
    halo = BF16_ROWS

    @pl.when(pl.program_id(1) == 0)
    def _():
        first = (i % tiles_per_seq) == 0
        last = ((i + 1) % tiles_per_seq) == 0
        gain = g_ref[...]
        h_ref[0:halo, :] = jnp.where(first, 0.0, _rmsnorm(xp_ref[...], gain)).astype(BF16)
        h_ref[halo:halo + tm, :] = _rmsnorm(x_ref[...], gain).astype(BF16)
        h_ref[halo + tm:, :] = jnp.where(last, 0.0, _rmsnorm(xn_ref[...], gain)).astype(BF16)

    def conv(r_ref, w_ref, cw_ref, cb_ref):
        r_ref[...] = jnp.dot(h_ref[...], w_ref[...], preferred_element_type=F32)
        return (r_ref[halo - 1:halo - 1 + tm, :] * cw_ref[0:1, :]
                + r_ref[halo:halo + tm, :] * cw_ref[1:2, :]
                + r_ref[halo + 1:halo + 1 + tm, :] * cw_ref[2:3, :] + cb_ref[...])

    a = conv(ra_ref, wa_ref, cwa_ref, cba_ref)
    b = conv(rb_ref, wb_ref, cwb_ref, cbb_ref)
    o_ref[...] = ((a / (1.0 + jnp.exp(-a))) * b).astype(o_ref.dtype)


def _norm_up_gate(x2d, gain, w_up, conv_w, conv_b, *, seq_len, tm, tn):
    T, D = x2d.shape
    F = w_up.shape[1] // 2
    assert seq_len % tm == 0 and F % tn == 0
    nj = F // tn
    prev_map, next_map = _halo_maps(tm, T, BF16_ROWS)
    cb = conv_b.reshape(1, 2 * F)

    def chan(rows, off):
        return pl.BlockSpec((rows, tn), lambda i, j: (0, off + j))

    return pl.pallas_call(
        functools.partial(_norm_up_gate_kernel, tiles_per_seq=seq_len // tm),
        grid=(T // tm, nj),
        in_specs=[
            pl.BlockSpec((tm, D), lambda i, j: (i, 0)),
            pl.BlockSpec((BF16_ROWS, D), lambda i, j: (prev_map(i), 0)),
            pl.BlockSpec((BF16_ROWS, D), lambda i, j: (next_map(i), 0)),
            pl.BlockSpec((1, D), lambda i, j: (0, 0)),
            chan(D, 0), chan(D, nj), chan(3, 0), chan(3, nj), chan(1, 0), chan(1, nj),
        ],
        out_specs=pl.BlockSpec((tm, tn), lambda i, j: (i, j)),
        out_shape=jax.ShapeDtypeStruct((T, F), BF16),
        scratch_shapes=[pltpu.VMEM((tm + 2 * BF16_ROWS, D), BF16),
                        pltpu.VMEM((tm + 2 * BF16_ROWS, tn), F32),
                        pltpu.VMEM((tm + 2 * BF16_ROWS, tn), F32)],
        compiler_params=_params(("parallel", "arbitrary")),
        name="norm_up_gate",
    )(x2d, x2d, x2d, gain.reshape(1, D), w_up, w_up, conv_w, conv_w, cb, cb)


def _down_residual_kernel(g_ref, w_ref, x_ref, gain_ref, o_ref, *, final_norm):
    k = pl.program_id(1)
    part = jnp.dot(g_ref[...], w_ref[...], preferred_element_type=F32)

    @pl.when(k == 0)
    def _():
        o_ref[...] = x_ref[...] + part

    @pl.when(k > 0)
    def _():
        o_ref[...] += part

    if final_norm:
        @pl.when(k == pl.num_programs(1) - 1)
        def _():
            o_ref[...] = _rmsnorm(o_ref[...], gain_ref[...])


def _down_residual(g, w_down, x2d, final_gain, *, tm, tk):
    T, D = x2d.shape
    F = w_down.shape[0]
    assert T % tm == 0 and F % tk == 0
    final_norm = final_gain is not None
    gain = final_gain if final_norm else jnp.ones((D,), F32)
    return pl.pallas_call(
        functools.partial(_down_residual_kernel, final_norm=final_norm),
        grid=(T // tm, F // tk),
        in_specs=[
            pl.BlockSpec((tm, tk), lambda i, k: (i, k)),
            pl.BlockSpec((tk, D), lambda i, k: (k, 0)),
            pl.BlockSpec((tm, D), lambda i, k: (i, 0)),
            pl.BlockSpec((1, D), lambda i, k: (0, 0)),
        ],
        out_specs=pl.BlockSpec((tm, D), lambda i, k: (i, 0)),
        out_shape=jax.ShapeDtypeStruct((T, D), F32),
        compiler_params=_params(("parallel", "arbitrary")),
        name="down_residual",
    )(g, w_down, x2d, gain.reshape(1, D))


def _banded_attention_kernel(slope_ref, q_ref, k_ref, kp_ref, kn_ref, v_ref, vp_ref, vn_ref,
                             o_ref, lse_ref, kbuf, vbuf, *, dil, seq, half, qs):
    lq = q_ref.shape[1]
    dh = q_ref.shape[2] // ATTN_HEADS
    it, hg = pl.program_id(1), pl.program_id(2)
    for buf, prev, main, nxt in ((kbuf, kp_ref, k_ref, kn_ref), (vbuf, vp_ref, v_ref, vn_ref)):
        buf[:, 0:half, :] = prev[...]
        buf[:, half:half + lq, :] = main[...]
        buf[:, half + lq:, :] = nxt[...]

    win = qs + 2 * half
    ii = lax.broadcasted_iota(jnp.int32, (qs, win), 0)
    jj = lax.broadcasted_iota(jnp.int32, (qs, win), 1)
    delta = jnp.abs(jj - half - ii)
    band = delta <= half
    dist = (dil * delta).astype(F32)
    lane = lax.broadcasted_iota(jnp.int32, (qs, LANES), 1)
    scale = dh ** -0.5
    neg_slopes = [-slope_ref[hg * ATTN_HEADS + hh] for hh in range(ATTN_HEADS)]

    for sb in range(lq // qs):
        key_pos = it * lq + (sb * qs - half) + jj
        valid = band & (key_pos >= 0) & (key_pos < seq)
        for r in range(dil):
            rows = pl.ds(sb * qs, qs) if dil == 1 else pl.ds(r + sb * qs * dil, qs, stride=dil)
            lse_tile = jnp.zeros((qs, LANES), F32)
            for hh in range(ATTN_HEADS):
                cols = slice(hh * dh, (hh + 1) * dh)
                q = q_ref[r, sb * qs:(sb + 1) * qs, cols]
                kw = kbuf[r, sb * qs:sb * qs + win, cols]
                vw = vbuf[r, sb * qs:sb * qs + win, cols]
                s = lax.dot_general(q, kw, (((1,), (1,)), ((), ())),
                                    preferred_element_type=F32) * scale
                s = jnp.where(valid, s + neg_slopes[hh] * dist, NEG_INF)
                m = jnp.max(s, axis=-1, keepdims=True)
                p = jnp.exp(s - m)
                den = jnp.sum(p, axis=-1, keepdims=True)
                o = jnp.dot(p.astype(BF16), vw, preferred_element_type=F32)
                o_ref[hh, rows, :] = o / den
                lse_tile = jnp.where(lane == hh, m + jnp.log(den), lse_tile)
            lse_ref[rows, :] = lse_tile


def _banded_attention(qkv, slopes, *, window, dil):
    B, d, L, D3 = qkv.shape
    D = D3 // 3
    dh = D // N_HEADS
    half = (window // 2) // dil
    qs = LANES
    lq = max(ATTN_TOKENS // dil, qs)
    tokens = lq * dil
    wcols = ATTN_HEADS * dh
    n_hg = N_HEADS // ATTN_HEADS
    assert d == dil and L % lq == 0 and lq % half == 0 and half % BF16_ROWS == 0
    n_t = L // lq
    per = lq // half
    last = L // half - 1

    def main(c):
        return pl.BlockSpec((None, dil, lq, wcols), lambda b, i, g: (b, 0, i, c * n_hg + g))

    def prev(c):
        return pl.BlockSpec((None, dil, half, wcols),
                            lambda b, i, g: (b, 0, jnp.maximum(i * per - 1, 0), c * n_hg + g))

    def nxt(c):
        return pl.BlockSpec((None, dil, half, wcols),
                            lambda b, i, g: (b, 0, jnp.minimum((i + 1) * per, last), c * n_hg + g))

    kern = functools.partial(_banded_attention_kernel, dil=dil, seq=L, half=half, qs=qs)
    T = B * L * dil
    return pl.pallas_call(
        kern,
        grid=(B, n_t, n_hg),
        in_specs=[pl.BlockSpec(memory_space=pltpu.SMEM),
                  main(0), main(1), prev(1), nxt(1), main(2), prev(2), nxt(2)],
        out_specs=[pl.BlockSpec((ATTN_HEADS, tokens, dh), lambda b, i, g: (g, b * n_t + i, 0)),
                   pl.BlockSpec((tokens, LANES), lambda b, i, g: (b * n_t + i, g))],
        out_shape=[jax.ShapeDtypeStruct((N_HEADS, T, dh), F32),
                   jax.ShapeDtypeStruct((T, n_hg * LANES), F32)],
        scratch_shapes=[pltpu.VMEM((dil, lq + 2 * half, wcols), BF16),
                        pltpu.VMEM((dil, lq + 2 * half, wcols), BF16)],
        compiler_params=_params(("parallel", "parallel", "parallel")),
        name="banded_attention",
    )(slopes, qkv, qkv, qkv, qkv, qkv, qkv, qkv)


def _combine_out_kernel(*refs, n_groups):
    o_refs = refs[:n_groups]
    l_refs = refs[n_groups:2 * n_groups]
    x_ref, w_ref, out_ref, mix_ref = refs[2 * n_groups:]
    dh = o_refs[0].shape[2]
    lses = [r[...] for r in l_refs]
    m = functools.reduce(jnp.maximum, lses)
    es = [jnp.exp(l - m) for l in lses]
    total = functools.reduce(lambda a, b: a + b, es)
    wts = [e / total for e in es]
    for h in range(N_HEADS):
        col = (h // ATTN_HEADS) * LANES + h % ATTN_HEADS
        mixed = functools.reduce(
            lambda a, b: a + b,
            [wt[:, col:col + 1] * o[h] for wt, o in zip(wts, o_refs)])
        mix_ref[:, h * dh:(h + 1) * dh] = mixed.astype(mix_ref.dtype)
    out_ref[...] = x_ref[...] + jnp.dot(mix_ref[...], w_ref[...],
                                        preferred_element_type=F32)


def _combine_out(outs, lses, x2d, w_out, *, tm):
    T, D = x2d.shape
    n = len(outs)
    dh = D // N_HEADS
    lw = lses[0].shape[1]
    return pl.pallas_call(
        functools.partial(_combine_out_kernel, n_groups=n),
        grid=(T // tm,),
        in_specs=([pl.BlockSpec((N_HEADS, tm, dh), lambda i: (0, i, 0))] * n
                  + [pl.BlockSpec((tm, lw), lambda i: (i, 0))] * n
                  + [pl.BlockSpec((tm, D), lambda i: (i, 0)),
                     pl.BlockSpec((D, D), lambda i: (0, 0))]),
        out_specs=pl.BlockSpec((tm, D), lambda i: (i, 0)),
        out_shape=jax.ShapeDtypeStruct((T, D), F32),
        scratch_shapes=[pltpu.VMEM((tm, D), BF16)],
        compiler_params=_params(("parallel",)),
        name="combine_out",
    )(*outs, *lses, x2d, w_out)


def kernel(x, mix_norm_g, ffn_norm_g, final_norm_g, sc_w_in, sc_conv_w, sc_conv_b, sc_w_out,
           attn_w_qkv, attn_w_out, ffn_w_up, ffn_conv_w, ffn_conv_b, ffn_w_down):
    B, S, D = x.shape
    T = B * S
    depth = mix_norm_g.shape[0]
    n_mixers = 2
    slopes = jnp.asarray(2.0 ** (-ALIBI_MAX * np.arange(1, N_HEADS + 1) / N_HEADS), dtype=F32)
    xf = x.reshape(T, D)
    for i in range(depth):
        j = i // n_mixers
        if i % n_mixers == 0:
            z = _norm_matmul(xf, mix_norm_g[i], sc_w_in[j].astype(BF16), batch=B, dil=1,
                             col0=0, n_cols=3 * D, tm=1024, tn=1024).reshape(T, 3 * D)
            xf = _shortconv_out(z, xf, sc_conv_w[j], sc_conv_b[j], sc_w_out[j].astype(BF16),
                                seq_len=S, tm=256)
        else:
            w_qkv = attn_w_qkv[j].astype(BF16)
            outs, lses = [], []
            for g, (window, dil) in enumerate(DILATED_GROUPS):
                qkv = _norm_matmul(xf, mix_norm_g[i], w_qkv, batch=B, dil=dil,
                                   col0=g * 3 * D, n_cols=3 * D, tm=1024, tn=1024)
                o, lse = _banded_attention(qkv, slopes, window=window, dil=dil)
                outs.append(o)
                lses.append(lse)
            xf = _combine_out(outs, lses, xf, attn_w_out[j].astype(BF16), tm=256)
        g = _norm_up_gate(xf, ffn_norm_g[i], ffn_w_up[i].astype(BF16), ffn_conv_w[i],
                          ffn_conv_b[i], seq_len=S, tm=1024, tn=512)
        xf = _down_residual(g, ffn_w_down[i].astype(BF16), xf,
                            final_norm_g if i == depth - 1 else None, tm=512, tk=1408)
    return xf.reshape(B, S, D)
```

```python
import functools

import numpy as np
import jax
import jax.numpy as jnp
from jax import lax
from jax.experimental import pallas as pl
from jax.experimental.pallas import tpu as pltpu

N_HEADS = 16
DILATED_GROUPS = ((128, 1), (512, 4), (2048, 16))
NORM_EPS = 1e-5
ALIBI_MAX = 8.0
NEG_INF = -1e30
LOG2E = 1.4426950408889634

LANES = 128
BF16_ROWS = 16
MXU_DIM = 256
VMEM_LIMIT = 56 * 1024 * 1024
ATTN_TOKENS = 1024
ATTN_HEADS = 4

F32 = jnp.float32
BF16 = jnp.bfloat16


def _params(semantics):
    return pltpu.CompilerParams(dimension_semantics=semantics,
                                vmem_limit_bytes=VMEM_LIMIT)


def _rmsnorm(x, gain):
    ms = jnp.mean(x * x, axis=-1, keepdims=True)
    return x * lax.rsqrt(ms + NORM_EPS) * gain


def _log2(n):
    assert n & (n - 1) == 0
    return n.bit_length() - 1


def _norm_matmul_kernel(x_ref, g_ref, w_ref, o_ref, h_ref, *, dil):
    tm = x_ref.shape[0]
    per = tm // dil

    @pl.when(pl.program_id(1) == 0)
    def _():
        h = _rmsnorm(x_ref[...], g_ref[...]).astype(BF16)
        if dil == 1:
            h_ref[...] = h
        else:
            n = MXU_DIM // dil
            row = lax.broadcasted_iota(jnp.int32, (MXU_DIM, MXU_DIM), 0)
            col = lax.broadcasted_iota(jnp.int32, (MXU_DIM, MXU_DIM), 1)
            src = (row & (n - 1)) * dil + lax.shift_right_logical(row, _log2(n))
            perm = jnp.where(col == src, 1.0, 0.0).astype(BF16)
            for c in range(tm // MXU_DIM):
                hc = jnp.dot(perm, h[c * MXU_DIM:(c + 1) * MXU_DIM, :],
                             preferred_element_type=F32).astype(BF16)
                for r in range(dil):
                    h_ref[r * per + c * n:r * per + (c + 1) * n, :] = hc[r * n:(r + 1) * n, :]

    res = jnp.dot(h_ref[...], w_ref[...], preferred_element_type=F32).astype(o_ref.dtype)
    for r in range(dil):
        o_ref[r] = res[r * per:(r + 1) * per, :]


def _norm_matmul(x2d, gain, w, *, batch, dil, col0, n_cols, tm, tn):
    T, D = x2d.shape
    S = T // batch
    assert S % tm == 0 and n_cols % tn == 0 and col0 % tn == 0
    assert tm % MXU_DIM == 0 and (MXU_DIM // dil) % BF16_ROWS == 0
    tiles = S // tm
    return pl.pallas_call(
        functools.partial(_norm_matmul_kernel, dil=dil),
        grid=(T // tm, n_cols // tn),
        in_specs=[
            pl.BlockSpec((tm, D), lambda m, j: (m, 0)),
            pl.BlockSpec((1, D), lambda m, j: (0, 0)),
            pl.BlockSpec((D, tn), lambda m, j: (0, col0 // tn + j)),
        ],
        out_specs=pl.BlockSpec((None, dil, tm // dil, tn),
                               lambda m, j: (m // tiles, 0, m % tiles, j)),
        out_shape=jax.ShapeDtypeStruct((batch, dil, S // dil, n_cols), BF16),
        scratch_shapes=[pltpu.VMEM((tm, D), BF16)],
        compiler_params=_params(("parallel", "arbitrary")),
        name="norm_matmul",
    )(x2d, gain.reshape(1, D), w)


def _halo_maps(tm, n_rows, rows):
    per = tm // rows
    last = n_rows // rows - 1
    prev_map = lambda i: jnp.maximum(i * per - 1, 0)
    next_map = lambda i: jnp.minimum((i + 1) * per, last)
    return prev_map, next_map


def _dwconv3(v, prev_row, next_row, w, b):
    tm = v.shape[0]
    row = lax.broadcasted_iota(jnp.int32, v.shape, 0)
    v_prev = jnp.where(row == 0, prev_row, pltpu.roll(v, 1, 0))
    v_next = jnp.where(row == tm - 1, next_row, pltpu.roll(v, tm - 1, 0))
    return v_prev * w[0:1, :] + v * w[1:2, :] + v_next * w[2:3, :] + b


def _shortconv_out_kernel(z_ref, zp_ref, zn_ref, x_ref, cw_ref, cb_ref, w_ref, o_ref,
                          *, tiles_per_seq):
    i = pl.program_id(0)
    D = x_ref.shape[1]
    first = (i % tiles_per_seq) == 0
    last = ((i + 1) % tiles_per_seq) == 0

    def gated(ref):
        return ref[:, 2 * D:3 * D].astype(F32) * ref[:, 0:D].astype(F32)

    v = gated(z_ref)
    vp = jnp.where(first, 0.0, gated(zp_ref)[BF16_ROWS - 1:BF16_ROWS, :])
    vn = jnp.where(last, 0.0, gated(zn_ref)[0:1, :])
    y = z_ref[:, D:2 * D].astype(F32) * _dwconv3(v, vp, vn, cw_ref[...], cb_ref[...])
    o_ref[...] = x_ref[...] + jnp.dot(y.astype(BF16), w_ref[...],
                                      preferred_element_type=F32)


def _shortconv_out(z, x2d, conv_w, conv_b, w_out, *, seq_len, tm):
    T, D = x2d.shape
    assert seq_len % tm == 0
    prev_map, next_map = _halo_maps(tm, T, BF16_ROWS)
    kern = functools.partial(_shortconv_out_kernel, tiles_per_seq=seq_len // tm)
    return pl.pallas_call(
        kern,
        grid=(T // tm,),
        in_specs=[
            pl.BlockSpec((tm, 3 * D), lambda i: (i, 0)),
            pl.BlockSpec((BF16_ROWS, 3 * D), lambda i: (prev_map(i), 0)),
            pl.BlockSpec((BF16_ROWS, 3 * D), lambda i: (next_map(i), 0)),
            pl.BlockSpec((tm, D), lambda i: (i, 0)),
            pl.BlockSpec((3, D), lambda i: (0, 0)),
            pl.BlockSpec((1, D), lambda i: (0, 0)),
            pl.BlockSpec((D, D), lambda i: (0, 0)),
        ],
        out_specs=pl.BlockSpec((tm, D), lambda i: (i, 0)),
        out_shape=jax.ShapeDtypeStruct((T, D), F32),
        compiler_params=_params(("parallel",)),
        name="shortconv_out",
    )(z, z, z, x2d, conv_w, conv_b.reshape(1, D), w_out)


def _norm_up_gate_kernel(x_ref, xp_ref, xn_ref, g_ref, wa_ref, wb_ref, cwa_ref, cwb_ref,
                         cba_ref, cbb_ref, o_ref, h_ref, ra_ref, rb_ref, *, tiles_per_seq):
    i = pl.program_id(0)
    tm = x_ref.shape[0]
    halo = BF16_ROWS

    @pl.when(pl.program_id(1) == 0)
    def _():
        first = (i % tiles_per_seq) == 0
        last = ((i + 1) % tiles_per_seq) == 0
        gain = g_ref[...]
        h_ref[0:halo, :] = jnp.where(first, 0.0, _rmsnorm(xp_ref[...], gain)).astype(BF16)
        h_ref[halo:halo + tm, :] = _rmsnorm(x_ref[...], gain).astype(BF16)
        h_ref[halo + tm:, :] = jnp.where(last, 0.0, _rmsnorm(xn_ref[...], gain)).astype(BF16)

    def conv(r_ref, w_ref, cw_ref, cb_ref):
        r_ref[...] = jnp.dot(h_ref[...], w_ref[...], preferred_element_type=F32)
        return (r_ref[halo - 1:halo - 1 + tm, :] * cw_ref[0:1, :]
                + r_ref[halo:halo + tm, :] * cw_ref[1:2, :]
                + r_ref[halo + 1:halo + 1 + tm, :] * cw_ref[2:3, :] + cb_ref[...])

    a = conv(ra_ref, wa_ref, cwa_ref, cba_ref)
    b = conv(rb_ref, wb_ref, cwb_ref, cbb_ref)
    o_ref[...] = ((a / (1.0 + jnp.exp(-a))) * b).astype(o_ref.dtype)


def _norm_up_gate(x2d, gain, w_up, conv_w, conv_b, *, seq_len, tm, tn):
    T, D = x2d.shape
    F = w_up.shape[1] // 2
    assert seq_len % tm == 0 and F % tn == 0
    nj = F // tn
    prev_map, next_map = _halo_maps(tm, T, BF16_ROWS)
    cb = conv_b.reshape(1, 2 * F)

    def chan(rows, off):
        return pl.BlockSpec((rows, tn), lambda i, j: (0, off + j))

    return pl.pallas_call(
        functools.partial(_norm_up_gate_kernel, tiles_per_seq=seq_len // tm),
        grid=(T // tm, nj),
        in_specs=[
            pl.BlockSpec((tm, D), lambda i, j: (i, 0)),
            pl.BlockSpec((BF16_ROWS, D), lambda i, j: (prev_map(i), 0)),
            pl.BlockSpec((BF16_ROWS, D), lambda i, j: (next_map(i), 0)),
            pl.BlockSpec((1, D), lambda i, j: (0, 0)),
            chan(D, 0), chan(D, nj), chan(3, 0), chan(3, nj), chan(1, 0), chan(1, nj),
        ],
        out_specs=pl.BlockSpec((tm, tn), lambda i, j: (i, j)),
        out_shape=jax.ShapeDtypeStruct((T, F), BF16),
        scratch_shapes=[pltpu.VMEM((tm + 2 * BF16_ROWS, D), BF16),
                        pltpu.VMEM((tm + 2 * BF16_ROWS, tn), F32),
                        pltpu.VMEM((tm + 2 * BF16_ROWS, tn), F32)],
        compiler_params=_params(("parallel", "arbitrary")),
        name="norm_up_gate",
    )(x2d, x2d, x2d, gain.reshape(1, D), w_up, w_up, conv_w, conv_w, cb, cb)


def _down_residual_kernel(g_ref, w_ref, x_ref, gain_ref, o_ref, *, final_norm):
    y = x_ref[...] + jnp.dot(g_ref[...], w_ref[...], preferred_element_type=F32)
    o_ref[...] = _rmsnorm(y, gain_ref[...]) if final_norm else y


def _down_residual(g, w_down, x2d, final_gain, *, tm):
    T, D = x2d.shape
    F = w_down.shape[0]
    assert T % tm == 0
    final_norm = final_gain is not None
    gain = final_gain if final_norm else jnp.ones((D,), F32)
    return pl.pallas_call(
        functools.partial(_down_residual_kernel, final_norm=final_norm),
        grid=(T // tm,),
        in_specs=[
            pl.BlockSpec((tm, F), lambda i: (i, 0)),
            pl.BlockSpec((F, D), lambda i: (0, 0), pipeline_mode=pl.Buffered(1)),
            pl.BlockSpec((tm, D), lambda i: (i, 0)),
            pl.BlockSpec((1, D), lambda i: (0, 0)),
        ],
        out_specs=pl.BlockSpec((tm, D), lambda i: (i, 0)),
        out_shape=jax.ShapeDtypeStruct((T, D), F32),
        compiler_params=_params(("parallel",)),
        name="down_residual",
    )(g, w_down, x2d, gain.reshape(1, D))


def _banded_attention_kernel(slope_ref, q_ref, k_ref, kp_ref, kn_ref, v_ref, vp_ref, vn_ref,
                             acc_ref, m_ref, lse_ref, kbuf, vbuf, bias_ref,
                             *, dil, seq, half, qs):
    lq = q_ref.shape[1]
    dh = q_ref.shape[2] // ATTN_HEADS
    n_sb = lq // qs
    it, hg = pl.program_id(1), pl.program_id(2)
    for buf, prev, main, nxt in ((kbuf, kp_ref, k_ref, kn_ref), (vbuf, vp_ref, v_ref, vn_ref)):
        buf[:, 0:half, :] = prev[...]
        buf[:, half:half + lq, :] = main[...]
        buf[:, half + lq:, :] = nxt[...]

    win = qs + 2 * half
    ii = lax.broadcasted_iota(jnp.int32, (qs, win), 0)
    jj = lax.broadcasted_iota(jnp.int32, (qs, win), 1)
    delta = jnp.abs(jj - half - ii)
    dist = (dil * delta).astype(F32)
    lane = lax.broadcasted_iota(jnp.int32, (qs, LANES), 1)
    scale2 = dh ** -0.5 * LOG2E
    neg_slopes2 = [-slope_ref[hg * ATTN_HEADS + hh] * LOG2E for hh in range(ATTN_HEADS)]

    variants = sorted({(sb == 0, sb == n_sb - 1) for sb in range(n_sb)})
    for vi, (is_first, is_last) in enumerate(variants):
        masked = delta > half
        if is_first:
            masked = masked | ((it == 0) & (jj < half))
        if is_last:
            masked = masked | ((it == seq // lq - 1) & (jj >= qs + half))
        for hh in range(ATTN_HEADS):
            bias_ref[vi * ATTN_HEADS + hh] = jnp.where(masked, NEG_INF, neg_slopes2[hh] * dist)

    for sb in range(n_sb):
        vi = variants.index((sb == 0, sb == n_sb - 1))
        for r in range(dil):
            rows = pl.ds(sb * qs, qs) if dil == 1 else pl.ds(r + sb * qs * dil, qs, stride=dil)
            m_tile = jnp.zeros((qs, LANES), F32)
            den_tile = jnp.ones((qs, LANES), F32)
            for hh in range(ATTN_HEADS):
                cols = slice(hh * dh, (hh + 1) * dh)
                q = q_ref[r, sb * qs:(sb + 1) * qs, cols]
                kw = kbuf[r, sb * qs:sb * qs + win, cols]
                vw = vbuf[r, sb * qs:sb * qs + win, cols]
                t = lax.dot_general(q, kw, (((1,), (1,)), ((), ())),
                                    preferred_element_type=F32) * scale2
                t = t + bias_ref[vi * ATTN_HEADS + hh]
                m = jnp.max(t, axis=-1, keepdims=True)
                p = jnp.exp2(t - m)
                den = jnp.sum(p, axis=-1, keepdims=True)
                acc_ref[hh, rows, :] = jnp.dot(p.astype(BF16), vw, preferred_element_type=F32)
                m_tile = jnp.where(lane == hh, m, m_tile)
                den_tile = jnp.where(lane == hh, den, den_tile)
            m_ref[rows, :] = m_tile
            lse_ref[rows, :] = m_tile + jnp.log2(den_tile)


def _banded_attention(qkv, slopes, *, window, dil):
    B, d, L, D3 = qkv.shape
    D = D3 // 3
    dh = D // N_HEADS
    half = (window // 2) // dil
    qs = LANES
    lq = max(ATTN_TOKENS // dil, qs)
    tokens = lq * dil
    wcols = ATTN_HEADS * dh
    n_hg = N_HEADS // ATTN_HEADS
    assert d == dil and L % lq == 0 and lq % half == 0 and half % BF16_ROWS == 0
    n_t = L // lq
    per = lq // half
    last = L // half - 1

    def main(c):
        return pl.BlockSpec((None, dil, lq, wcols), lambda b, i, g: (b, 0, i, c * n_hg + g))

    def prev(c):
        return pl.BlockSpec((None, dil, half, wcols),
                            lambda b, i, g: (b, 0, jnp.maximum(i * per - 1, 0), c * n_hg + g))

    def nxt(c):
        return pl.BlockSpec((None, dil, half, wcols),
                            lambda b, i, g: (b, 0, jnp.minimum((i + 1) * per, last), c * n_hg + g))

    kern = functools.partial(_banded_attention_kernel, dil=dil, seq=L, half=half, qs=qs)
    T = B * L * dil
    return pl.pallas_call(
        kern,
        grid=(B, n_t, n_hg),
        in_specs=[pl.BlockSpec(memory_space=pltpu.SMEM),
                  main(0), main(1), prev(1), nxt(1), main(2), prev(2), nxt(2)],
        out_specs=[pl.BlockSpec((ATTN_HEADS, tokens, dh), lambda b, i, g: (g, b * n_t + i, 0)),
                   pl.BlockSpec((tokens, LANES), lambda b, i, g: (b * n_t + i, g)),
                   pl.BlockSpec((tokens, LANES), lambda b, i, g: (b * n_t + i, g))],
        out_shape=[jax.ShapeDtypeStruct((N_HEADS, T, dh), F32),
                   jax.ShapeDtypeStruct((T, n_hg * LANES), F32),
                   jax.ShapeDtypeStruct((T, n_hg * LANES), F32)],
        scratch_shapes=[pltpu.VMEM((dil, lq + 2 * half, wcols), BF16),
                        pltpu.VMEM((dil, lq + 2 * half, wcols), BF16),
                        pltpu.VMEM((min(lq // qs, 3) * ATTN_HEADS, qs, qs + 2 * half), F32)],
        compiler_params=_params(("parallel", "parallel", "parallel")),
        name="banded_attention",
    )(slopes, qkv, qkv, qkv, qkv, qkv, qkv, qkv)


def _combine_out_kernel(*refs, n_groups):
    acc_refs = refs[:n_groups]
    m_refs = refs[n_groups:2 * n_groups]
    l_refs = refs[2 * n_groups:3 * n_groups]
    x_ref, w_ref, out_ref, mix_ref = refs[3 * n_groups:]
    dh = acc_refs[0].shape[2]
    lses = [r[...] for r in l_refs]
    top = functools.reduce(jnp.maximum, lses)
    total = functools.reduce(lambda a, b: a + b, [jnp.exp2(l - top) for l in lses])
    lse_all = top + jnp.log2(total)
    coefs = [jnp.exp2(r[...] - lse_all) for r in m_refs]
    for h in range(N_HEADS):
        col = (h // ATTN_HEADS) * LANES + h % ATTN_HEADS
        mixed = functools.reduce(
            lambda a, b: a + b,
            [c[:, col:col + 1] * acc[h] for c, acc in zip(coefs, acc_refs)])
        mix_ref[:, h * dh:(h + 1) * dh] = mixed.astype(mix_ref.dtype)
    out_ref[...] = x_ref[...] + jnp.dot(mix_ref[...], w_ref[...],
                                        preferred_element_type=F32)


def _combine_out(accs, maxes, lses, x2d, w_out, *, tm):
    T, D = x2d.shape
    n = len(accs)
    dh = D // N_HEADS
    lw = lses[0].shape[1]
    return pl.pallas_call(
        functools.partial(_combine_out_kernel, n_groups=n),
        grid=(T // tm,),
        in_specs=([pl.BlockSpec((N_HEADS, tm, dh), lambda i: (0, i, 0))] * n
                  + [pl.BlockSpec((tm, lw), lambda i: (i, 0))] * (2 * n)
                  + [pl.BlockSpec((tm, D), lambda i: (i, 0)),
                     pl.BlockSpec((D, D), lambda i: (0, 0))]),
        out_specs=pl.BlockSpec((tm, D), lambda i: (i, 0)),
        out_shape=jax.ShapeDtypeStruct((T, D), F32),
        scratch_shapes=[pltpu.VMEM((tm, D), BF16)],
        compiler_params=_params(("parallel",)),
        name="combine_out",
    )(*accs, *maxes, *lses, x2d, w_out)


def kernel(x, mix_norm_g, ffn_norm_g, final_norm_g, sc_w_in, sc_conv_w, sc_conv_b, sc_w_out,
           attn_w_qkv, attn_w_out, ffn_w_up, ffn_conv_w, ffn_conv_b, ffn_w_down):
    B, S, D = x.shape
    T = B * S
    depth = mix_norm_g.shape[0]
    n_mixers = 2
    slopes = jnp.asarray(2.0 ** (-ALIBI_MAX * np.arange(1, N_HEADS + 1) / N_HEADS), dtype=F32)
    xf = x.reshape(T, D)
    for i in range(depth):
        j = i // n_mixers
        if i % n_mixers == 0:
            z = _norm_matmul(xf, mix_norm_g[i], sc_w_in[j].astype(BF16), batch=B, dil=1,
                             col0=0, n_cols=3 * D, tm=1024, tn=1024).reshape(T, 3 * D)
            xf = _shortconv_out(z, xf, sc_conv_w[j], sc_conv_b[j], sc_w_out[j].astype(BF16),
                                seq_len=S, tm=256)
        else:
            w_qkv = attn_w_qkv[j].astype(BF16)
            parts = []
            for g, (window, dil) in enumerate(DILATED_GROUPS):
                qkv = _norm_matmul(xf, mix_norm_g[i], w_qkv, batch=B, dil=dil,
                                   col0=g * 3 * D, n_cols=3 * D, tm=1024, tn=1024)
                parts.append(_banded_attention(qkv, slopes, window=window, dil=dil))
            accs, maxes, lses = zip(*parts)
            xf = _combine_out(accs, maxes, lses, xf, attn_w_out[j].astype(BF16), tm=256)
        g = _norm_up_gate(xf, ffn_norm_g[i], ffn_w_up[i].astype(BF16), ffn_conv_w[i],
                          ffn_conv_b[i], seq_len=S, tm=1024, tn=512)
        xf = _down_residual(g, ffn_w_down[i].astype(BF16), xf,
                            final_norm_g if i == depth - 1 else None, tm=512)
    return xf.reshape(B, S, D)
```

```python
import functools

import numpy as np
import jax
import jax.numpy as jnp
from jax import lax
from jax.experimental import pallas as pl
from jax.experimental.pallas import tpu as pltpu

N_HEADS = 16
DILATED_GROUPS = ((128, 1), (512, 4), (2048, 16))
NORM_EPS = 1e-5
ALIBI_MAX = 8.0
NEG_INF = -1e30
LOG2E = 1.4426950408889634

LANES = 128
BF16_ROWS = 16
MXU_DIM = 256
VMEM_LIMIT = 56 * 1024 * 1024
ATTN_TOKENS = 1024
ATTN_HEADS = 4

F32 = jnp.float32
BF16 = jnp.bfloat16


def _params(semantics):
    return pltpu.CompilerParams(dimension_semantics=semantics,
                                vmem_limit_bytes=VMEM_LIMIT)


def _rmsnorm(x, gain):
    ms = jnp.mean(x * x, axis=-1, keepdims=True)
    return x * lax.rsqrt(ms + NORM_EPS) * gain


def _log2(n):
    assert n & (n - 1) == 0
    return n.bit_length() - 1


def _norm_matmul_kernel(x_ref, g_ref, w_ref, o_ref, h_ref, *, dil):
    tm = x_ref.shape[0]
    per = tm // dil

    @pl.when(pl.program_id(1) == 0)
    def _():
        h = _rmsnorm(x_ref[...], g_ref[...]).astype(BF16)
        if dil == 1:
            h_ref[...] = h
        else:
            n = MXU_DIM // dil
            row = lax.broadcasted_iota(jnp.int32, (MXU_DIM, MXU_DIM), 0)
            col = lax.broadcasted_iota(jnp.int32, (MXU_DIM, MXU_DIM), 1)
            src = (row & (n - 1)) * dil + lax.shift_right_logical(row, _log2(n))
            perm = jnp.where(col == src, 1.0, 0.0).astype(BF16)
            for c in range(tm // MXU_DIM):
                hc = jnp.dot(perm, h[c * MXU_DIM:(c + 1) * MXU_DIM, :],
                             preferred_element_type=F32).astype(BF16)
                for r in range(dil):
                    h_ref[r * per + c * n:r * per + (c + 1) * n, :] = hc[r * n:(r + 1) * n, :]

    res = jnp.dot(h_ref[...], w_ref[...], preferred_element_type=F32).astype(o_ref.dtype)
    for r in range(dil):
        o_ref[r] = res[r * per:(r + 1) * per, :]


def _norm_matmul(x2d, gain, w, *, batch, dil, col0, n_cols, tm, tn):
    T, D = x2d.shape
    S = T // batch
    assert S % tm == 0 and n_cols % tn == 0 and col0 % tn == 0
    assert tm % MXU_DIM == 0 and (MXU_DIM // dil) % BF16_ROWS == 0
    tiles = S // tm
    return pl.pallas_call(
        functools.partial(_norm_matmul_kernel, dil=dil),
        grid=(T // tm, n_cols // tn),
        in_specs=[
            pl.BlockSpec((tm, D), lambda m, j: (m, 0)),
            pl.BlockSpec((1, D), lambda m, j: (0, 0)),
            pl.BlockSpec((D, tn), lambda m, j: (0, col0 // tn + j)),
        ],
        out_specs=pl.BlockSpec((None, dil, tm // dil, tn),
                               lambda m, j: (m // tiles, 0, m % tiles, j)),
        out_shape=jax.ShapeDtypeStruct((batch, dil, S // dil, n_cols), BF16),
        scratch_shapes=[pltpu.VMEM((tm, D), BF16)],
        compiler_params=_params(("parallel", "arbitrary")),
        name="norm_matmul",
    )(x2d, gain.reshape(1, D), w)


def _halo_maps(tm, n_rows, rows):
    per = tm // rows
    last = n_rows // rows - 1
    prev_map = lambda i: jnp.maximum(i * per - 1, 0)
    next_map = lambda i: jnp.minimum((i + 1) * per, last)
    return prev_map, next_map


def _dwconv3(v, prev_row, next_row, w, b):
    tm = v.shape[0]
    row = lax.broadcasted_iota(jnp.int32, v.shape, 0)
    v_prev = jnp.where(row == 0, prev_row, pltpu.roll(v, 1, 0))
    v_next = jnp.where(row == tm - 1, next_row, pltpu.roll(v, tm - 1, 0))
    return v_prev * w[0:1, :] + v * w[1:2, :] + v_next * w[2:3, :] + b


def _shortconv_out_kernel(z_ref, zp_ref, zn_ref, x_ref, cw_ref, cb_ref, w_ref, o_ref,
                          *, tiles_per_seq):
    i = pl.program_id(0)
    D = x_ref.shape[1]
    first = (i % tiles_per_seq) == 0
    last = ((i + 1) % tiles_per_seq) == 0

    def gated(ref):
        return ref[:, 2 * D:3 * D].astype(F32) * ref[:, 0:D].astype(F32)

    v = gated(z_ref)
    vp = jnp.where(first, 0.0, gated(zp_ref)[BF16_ROWS - 1:BF16_ROWS, :])
    vn = jnp.where(last, 0.0, gated(zn_ref)[0:1, :])
    y = z_ref[:, D:2 * D].astype(F32) * _dwconv3(v, vp, vn, cw_ref[...], cb_ref[...])
    o_ref[...] = x_ref[...] + jnp.dot(y.astype(BF16), w_ref[...],
                                      preferred_element_type=F32)


def _shortconv_out(z, x2d, conv_w, conv_b, w_out, *, seq_len, tm):
    T, D = x2d.shape
    assert seq_len % tm == 0
    prev_map, next_map = _halo_maps(tm, T, BF16_ROWS)
    kern = functools.partial(_shortconv_out_kernel, tiles_per_seq=seq_len // tm)
    return pl.pallas_call(
        kern,
        grid=(T // tm,),
        in_specs=[
            pl.BlockSpec((tm, 3 * D), lambda i: (i, 0)),
            pl.BlockSpec((BF16_ROWS, 3 * D), lambda i: (prev_map(i), 0)),
            pl.BlockSpec((BF16_ROWS, 3 * D), lambda i: (next_map(i), 0)),
            pl.BlockSpec((tm, D), lambda i: (i, 0)),
            pl.BlockSpec((3, D), lambda i: (0, 0)),
            pl.BlockSpec((1, D), lambda i: (0, 0)),
            pl.BlockSpec((D, D), lambda i: (0, 0)),
        ],
        out_specs=pl.BlockSpec((tm, D), lambda i: (i, 0)),
        out_shape=jax.ShapeDtypeStruct((T, D), F32),
        compiler_params=_params(("parallel",)),
        name="shortconv_out",
    )(z, z, z, x2d, conv_w, conv_b.reshape(1, D), w_out)


def _norm_up_gate_kernel(x_ref, xp_ref, xn_ref, g_ref, wa_ref, wb_ref, cwa_ref, cwb_ref,
                         cba_ref, cbb_ref, o_ref, h_ref, ra_ref, rb_ref, *, tiles_per_seq):
    i = pl.program_id(0)
    tm = x_ref.shape[0]
    halo = BF16_ROWS

    @pl.when(pl.program_id(1) == 0)
    def _():
        first = (i % tiles_per_seq) == 0
        last = ((i + 1) % tiles_per_seq) == 0
        gain = g_ref[...]
        h_ref[0:halo, :] = jnp.where(first, 0.0, _rmsnorm(xp_ref[...], gain)).astype(BF16)
        h_ref[halo:halo + tm, :] = _rmsnorm(x_ref[...], gain).astype(BF16)
        h_ref[halo + tm:, :] = jnp.where(last, 0.0, _rmsnorm(xn_ref[...], gain)).astype(BF16)

    def conv(r_ref, w_ref, cw_ref, cb_ref):
        r_ref[...] = jnp.dot(h_ref[...], w_ref[...], preferred_element_type=F32)
        return (r_ref[halo - 1:halo - 1 + tm, :] * cw_ref[0:1, :]
                + r_ref[halo:halo + tm, :] * cw_ref[1:2, :]
                + r_ref[halo + 1:halo + 1 + tm, :] * cw_ref[2:3, :] + cb_ref[...])

    a = conv(ra_ref, wa_ref, cwa_ref, cba_ref)
    b = conv(rb_ref, wb_ref, cwb_ref, cbb_ref)
    o_ref[...] = ((a / (1.0 + jnp.exp(-a))) * b).astype(o_ref.dtype)


def _norm_up_gate(x2d, gain, w_up, conv_w, conv_b, *, seq_len, tm, tn):
    T, D = x2d.shape
    F = w_up.shape[1] // 2
    assert seq_len % tm == 0 and F % tn == 0
    nj = F // tn
    prev_map, next_map = _halo_maps(tm, T, BF16_ROWS)
    cb = conv_b.reshape(1, 2 * F)

    def chan(rows, off):
        return pl.BlockSpec((rows, tn), lambda i, j: (0, off + j))

    return pl.pallas_call(
        functools.partial(_norm_up_gate_kernel, tiles_per_seq=seq_len // tm),
        grid=(T // tm, nj),
        in_specs=[
            pl.BlockSpec((tm, D), lambda i, j: (i, 0)),
            pl.BlockSpec((BF16_ROWS, D), lambda i, j: (prev_map(i), 0)),
            pl.BlockSpec((BF16_ROWS, D), lambda i, j: (next_map(i), 0)),
            pl.BlockSpec((1, D), lambda i, j: (0, 0)),
            chan(D, 0), chan(D, nj), chan(3, 0), chan(3, nj), chan(1, 0), chan(1, nj),
        ],
        out_specs=pl.BlockSpec((tm, tn), lambda i, j: (i, j)),
        out_shape=jax.ShapeDtypeStruct((T, F), BF16),
        scratch_shapes=[pltpu.VMEM((tm + 2 * BF16_ROWS, D), BF16),
                        pltpu.VMEM((tm + 2 * BF16_ROWS, tn), F32),
                        pltpu.VMEM((tm + 2 * BF16_ROWS, tn), F32)],
        compiler_params=_params(("parallel", "arbitrary")),
        name="norm_up_gate",
    )(x2d, x2d, x2d, gain.reshape(1, D), w_up, w_up, conv_w, conv_w, cb, cb)


def _down_residual_kernel(g_ref, w_ref, x_ref, gain_ref, o_ref, *, final_norm):
    y = x_ref[...] + jnp.dot(g_ref[...], w_ref[...], preferred_element_type=F32)
    o_ref[...] = _rmsnorm(y, gain_ref[...]) if final_norm else y


def _down_residual(g, w_down, x2d, final_gain, *, tm):
    T, D = x2d.shape
    F = w_down.shape[0]
    assert T % tm == 0
    final_norm = final_gain is not None
    gain = final_gain if final_norm else jnp.ones((D,), F32)
    return pl.pallas_call(
        functools.partial(_down_residual_kernel, final_norm=final_norm),
        grid=(T // tm,),
        in_specs=[
            pl.BlockSpec((tm, F), lambda i: (i, 0)),
            pl.BlockSpec((F, D), lambda i: (0, 0), pipeline_mode=pl.Buffered(1)),
            pl.BlockSpec((tm, D), lambda i: (i, 0)),
            pl.BlockSpec((1, D), lambda i: (0, 0)),
        ],
        out_specs=pl.BlockSpec((tm, D), lambda i: (i, 0)),
        out_shape=jax.ShapeDtypeStruct((T, D), F32),
        compiler_params=_params(("parallel",)),
        name="down_residual",
    )(g, w_down, x2d, gain.reshape(1, D))


def _banded_attention_kernel(slope_ref, q_ref, k_ref, kp_ref, kn_ref, v_ref, vp_ref, vn_ref,
                             acc_ref, m_ref, lse_ref, kbuf, vbuf, bias_ref, slab_ref,
                             mstat_ref, lstat_ref, *, dil, seq, half, qs):
    lq = q_ref.shape[1]
    dh = q_ref.shape[2] // ATTN_HEADS
    n_sb = lq // qs
    it, hg = pl.program_id(1), pl.program_id(2)

    @pl.when(hg == 0)
    def _():
        mstat_ref[...] = jnp.zeros(mstat_ref.shape, F32)
        lstat_ref[...] = jnp.zeros(lstat_ref.shape, F32)

    for buf, prev, main, nxt in ((kbuf, kp_ref, k_ref, kn_ref), (vbuf, vp_ref, v_ref, vn_ref)):
        buf[:, 0:half, :] = prev[...]
        buf[:, half:half + lq, :] = main[...]
        buf[:, half + lq:, :] = nxt[...]

    win = qs + 2 * half
    ii = lax.broadcasted_iota(jnp.int32, (qs, win), 0)
    jj = lax.broadcasted_iota(jnp.int32, (qs, win), 1)
    delta = jnp.abs(jj - half - ii)
    dist = (dil * delta).astype(F32)
    lane = lax.broadcasted_iota(jnp.int32, (qs, LANES), 1)
    head0 = hg * ATTN_HEADS
    own_lanes = (lane >= head0) & (lane < head0 + ATTN_HEADS)
    scale2 = dh ** -0.5 * LOG2E
    neg_slopes2 = [-slope_ref[head0 + hh] * LOG2E for hh in range(ATTN_HEADS)]

    variants = sorted({(sb == 0, sb == n_sb - 1) for sb in range(n_sb)})
    for vi, (is_first, is_last) in enumerate(variants):
        masked = delta > half
        if is_first:
            masked = masked | ((it == 0) & (jj < half))
        if is_last:
            masked = masked | ((it == seq // lq - 1) & (jj >= qs + half))
        for hh in range(ATTN_HEADS):
            bias_ref[vi * ATTN_HEADS + hh] = jnp.where(masked, NEG_INF, neg_slopes2[hh] * dist)

    for sb in range(n_sb):
        vi = variants.index((sb == 0, sb == n_sb - 1))
        for r in range(dil):
            rows = pl.ds(sb * qs, qs) if dil == 1 else pl.ds(r + sb * qs * dil, qs, stride=dil)
            m_tile = jnp.zeros((qs, LANES), F32)
            den_tile = jnp.ones((qs, LANES), F32)
            for hh in range(ATTN_HEADS):
                cols = slice(hh * dh, (hh + 1) * dh)
                q = q_ref[r, sb * qs:(sb + 1) * qs, cols]
                kw = kbuf[r, sb * qs:sb * qs + win, cols]
                vw = vbuf[r, sb * qs:sb * qs + win, cols]
                t = lax.dot_general(q, kw, (((1,), (1,)), ((), ())),
                                    preferred_element_type=F32) * scale2
                t = t + bias_ref[vi * ATTN_HEADS + hh]
                m = jnp.max(t, axis=-1, keepdims=True)
                p = jnp.exp2(t - m)
                den = jnp.sum(p, axis=-1, keepdims=True)
                slab_ref[hh, rows, :] = jnp.dot(p.astype(BF16), vw, preferred_element_type=F32)
                m_tile = jnp.where(lane == head0 + hh, m, m_tile)
                den_tile = jnp.where(lane == head0 + hh, den, den_tile)
            srows = pl.ds(r * lq + sb * qs, qs)
            mstat_ref[srows, :] = jnp.where(own_lanes, m_tile, mstat_ref[srows, :])
            lstat_ref[srows, :] = jnp.where(own_lanes, m_tile + jnp.log2(den_tile),
                                            lstat_ref[srows, :])

    for hh in range(ATTN_HEADS):
        acc_ref[:, hh * dh:(hh + 1) * dh] = slab_ref[hh].astype(acc_ref.dtype)

    @pl.when(hg == pl.num_programs(2) - 1)
    def _():
        for r in range(dil):
            rows = pl.ds(0, lq) if dil == 1 else pl.ds(r, lq, stride=dil)
            m_ref[rows, :] = mstat_ref[r * lq:(r + 1) * lq, :]
            lse_ref[rows, :] = lstat_ref[r * lq:(r + 1) * lq, :]


def _banded_attention(qkv, slopes, *, window, dil):
    B, d, L, D3 = qkv.shape
    D = D3 // 3
    dh = D // N_HEADS
    half = (window // 2) // dil
    qs = LANES
    lq = max(ATTN_TOKENS // dil, qs)
    tokens = lq * dil
    wcols = ATTN_HEADS * dh
    n_hg = N_HEADS // ATTN_HEADS
    assert d == dil and L % lq == 0 and lq % half == 0 and half % BF16_ROWS == 0
    n_t = L // lq
    per = lq // half
    last = L // half - 1

    def main(c):
        return pl.BlockSpec((None, dil, lq, wcols), lambda b, i, g: (b, 0, i, c * n_hg + g))

    def prev(c):
        return pl.BlockSpec((None, dil, half, wcols),
                            lambda b, i, g: (b, 0, jnp.maximum(i * per - 1, 0), c * n_hg + g))

    def nxt(c):
        return pl.BlockSpec((None, dil, half, wcols),
                            lambda b, i, g: (b, 0, jnp.minimum((i + 1) * per, last), c * n_hg + g))

    kern = functools.partial(_banded_attention_kernel, dil=dil, seq=L, half=half, qs=qs)
    T = B * L * dil
    return pl.pallas_call(
        kern,
        grid=(B, n_t, n_hg),
        in_specs=[pl.BlockSpec(memory_space=pltpu.SMEM),
                  main(0), main(1), prev(1), nxt(1), main(2), prev(2), nxt(2)],
        out_specs=[pl.BlockSpec((tokens, wcols), lambda b, i, g: (b * n_t + i, g)),
                   pl.BlockSpec((tokens, LANES), lambda b, i, g: (b * n_t + i, 0)),
                   pl.BlockSpec((tokens, LANES), lambda b, i, g: (b * n_t + i, 0))],
        out_shape=[jax.ShapeDtypeStruct((T, D), BF16),
                   jax.ShapeDtypeStruct((T, LANES), F32),
                   jax.ShapeDtypeStruct((T, LANES), F32)],
        scratch_shapes=[pltpu.VMEM((dil, lq + 2 * half, wcols), BF16),
                        pltpu.VMEM((dil, lq + 2 * half, wcols), BF16),
                        pltpu.VMEM((min(lq // qs, 3) * ATTN_HEADS, qs, qs + 2 * half), F32),
                        pltpu.VMEM((ATTN_HEADS, tokens, dh), F32),
                        pltpu.VMEM((tokens, LANES), F32),
                        pltpu.VMEM((tokens, LANES), F32)],
        compiler_params=_params(("parallel", "parallel", "arbitrary")),
        name="banded_attention",
    )(slopes, qkv, qkv, qkv, qkv, qkv, qkv, qkv)


def _combine_out_kernel(*refs, n_groups):
    acc_refs = refs[:n_groups]
    m_refs = refs[n_groups:2 * n_groups]
    l_refs = refs[2 * n_groups:3 * n_groups]
    x_ref, w_ref, out_ref, mix_ref = refs[3 * n_groups:]
    dh = x_ref.shape[1] // N_HEADS
    lses = [r[...] for r in l_refs]
    top = functools.reduce(jnp.maximum, lses)
    total = functools.reduce(lambda a, b: a + b, [jnp.exp2(l - top) for l in lses])
    lse_all = top + jnp.log2(total)
    coefs = [jnp.exp2(r[...] - lse_all) for r in m_refs]
    for h in range(N_HEADS):
        cols = slice(h * dh, (h + 1) * dh)
        mixed = functools.reduce(
            lambda a, b: a + b,
            [c[:, h:h + 1] * acc[:, cols].astype(F32) for c, acc in zip(coefs, acc_refs)])
        mix_ref[:, cols] = mixed.astype(mix_ref.dtype)
    out_ref[...] = x_ref[...] + jnp.dot(mix_ref[...], w_ref[...],
                                        preferred_element_type=F32)


def _combine_out(accs, maxes, lses, x2d, w_out, *, tm):
    T, D = x2d.shape
    n = len(accs)
    return pl.pallas_call(
        functools.partial(_combine_out_kernel, n_groups=n),
        grid=(T // tm,),
        in_specs=([pl.BlockSpec((tm, D), lambda i: (i, 0))] * n
                  + [pl.BlockSpec((tm, LANES), lambda i: (i, 0))] * (2 * n)
                  + [pl.BlockSpec((tm, D), lambda i: (i, 0)),
                     pl.BlockSpec((D, D), lambda i: (0, 0))]),
        out_specs=pl.BlockSpec((tm, D), lambda i: (i, 0)),
        out_shape=jax.ShapeDtypeStruct((T, D), F32),
        scratch_shapes=[pltpu.VMEM((tm, D), BF16)],
        compiler_params=_params(("parallel",)),
        name="combine_out",
    )(*accs, *maxes, *lses, x2d, w_out)


def kernel(x, mix_norm_g, ffn_norm_g, final_norm_g, sc_w_in, sc_conv_w, sc_conv_b, sc_w_out,
           attn_w_qkv, attn_w_out, ffn_w_up, ffn_conv_w, ffn_conv_b, ffn_w_down):
    B, S, D = x.shape
    T = B * S
    depth = mix_norm_g.shape[0]
    n_mixers = 2
    slopes = jnp.asarray(2.0 ** (-ALIBI_MAX * np.arange(1, N_HEADS + 1) / N_HEADS), dtype=F32)
    xf = x.reshape(T, D)
    for i in range(depth):
        j = i // n_mixers
        if i % n_mixers == 0:
            z = _norm_matmul(xf, mix_norm_g[i], sc_w_in[j].astype(BF16), batch=B, dil=1,
                             col0=0, n_cols=3 * D, tm=1024, tn=1024).reshape(T, 3 * D)
            xf = _shortconv_out(z, xf, sc_conv_w[j], sc_conv_b[j], sc_w_out[j].astype(BF16),
                                seq_len=S, tm=256)
        else:
            w_qkv = attn_w_qkv[j].astype(BF16)
            parts = []
            for g, (window, dil) in enumerate(DILATED_GROUPS):
                qkv = _norm_matmul(xf, mix_norm_g[i], w_qkv, batch=B, dil=dil,
                                   col0=g * 3 * D, n_cols=3 * D, tm=1024, tn=1024)
                parts.append(_banded_attention(qkv, slopes, window=window, dil=dil))
            accs, maxes, lses = zip(*parts)
            xf = _combine_out(accs, maxes, lses, xf, attn_w_out[j].astype(BF16), tm=256)
        g = _norm_up_gate(xf, ffn_norm_g[i], ffn_w_up[i].astype(BF16), ffn_conv_w[i],
                          ffn_conv_b[i], seq_len=S, tm=1024, tn=512)
        xf = _down_residual(g, ffn_w_down[i].astype(BF16), xf,
                            final_norm_g if i == depth - 1 else None, tm=512)
    return xf.reshape(B, S, D)
```

```python
import functools

import numpy as np
import jax
import jax.numpy as jnp
from jax import lax
from jax.experimental import pallas as pl
from jax.experimental.pallas import tpu as pltpu

N_HEADS = 16
DILATED_GROUPS = ((128, 1), (512, 4), (2048, 16))
NORM_EPS = 1e-5
ALIBI_MAX = 8.0
NEG_INF = -1e30
LOG2E = 1.4426950408889634

LANES = 128
BF16_ROWS = 16
MXU_DIM = 256
VMEM_LIMIT = 56 * 1024 * 1024
ATTN_TOKENS = 1024
ATTN_HEADS = 4

F32 = jnp.float32
BF16 = jnp.bfloat16


def _params(semantics):
    return pltpu.CompilerParams(dimension_semantics=semantics,
                                vmem_limit_bytes=VMEM_LIMIT)


def _rmsnorm(x, gain):
    ms = jnp.mean(x * x, axis=-1, keepdims=True)
    return x * lax.rsqrt(ms + NORM_EPS) * gain


def _log2(n):
    assert n & (n - 1) == 0
    return n.bit_length() - 1


def _norm_matmul_kernel(x_ref, g_ref, w_ref, o_ref, h_ref, *, dil):
    tm = x_ref.shape[0]
    per = tm // dil

    @pl.when(pl.program_id(1) == 0)
    def _():
        h = _rmsnorm(x_ref[...], g_ref[...]).astype(BF16)
        if dil == 1:
            h_ref[...] = h
        else:
            n = MXU_DIM // dil
            row = lax.broadcasted_iota(jnp.int32, (MXU_DIM, MXU_DIM), 0)
            col = lax.broadcasted_iota(jnp.int32, (MXU_DIM, MXU_DIM), 1)
            src = (row & (n - 1)) * dil + lax.shift_right_logical(row, _log2(n))
            perm = jnp.where(col == src, 1.0, 0.0).astype(BF16)
            for c in range(tm // MXU_DIM):
                hc = jnp.dot(perm, h[c * MXU_DIM:(c + 1) * MXU_DIM, :],
                             preferred_element_type=F32).astype(BF16)
                for r in range(dil):
                    h_ref[r * per + c * n:r * per + (c + 1) * n, :] = hc[r * n:(r + 1) * n, :]

    res = jnp.dot(h_ref[...], w_ref[...].astype(BF16),
                  preferred_element_type=F32).astype(o_ref.dtype)
    for r in range(dil):
        o_ref[r] = res[r * per:(r + 1) * per, :]


def _norm_matmul(x2d, gain, w_all, layer, *, batch, dil, col0, n_cols, tm, tn):
    T, D = x2d.shape
    S = T // batch
    assert S % tm == 0 and n_cols % tn == 0 and col0 % tn == 0
    assert tm % MXU_DIM == 0 and (MXU_DIM // dil) % BF16_ROWS == 0
    tiles = S // tm
    return pl.pallas_call(
        functools.partial(_norm_matmul_kernel, dil=dil),
        grid=(T // tm, n_cols // tn),
        in_specs=[
            pl.BlockSpec((tm, D), lambda m, j: (m, 0)),
            pl.BlockSpec((1, D), lambda m, j: (0, 0)),
            pl.BlockSpec((None, D, tn), lambda m, j: (layer, 0, col0 // tn + j)),
        ],
        out_specs=pl.BlockSpec((None, dil, tm // dil, tn),
                               lambda m, j: (m // tiles, 0, m % tiles, j)),
        out_shape=jax.ShapeDtypeStruct((batch, dil, S // dil, n_cols), BF16),
        scratch_shapes=[pltpu.VMEM((tm, D), BF16)],
        compiler_params=_params(("parallel", "arbitrary")),
        name="norm_matmul",
    )(x2d, gain.reshape(1, D), w_all)


def _halo_maps(tm, n_rows, rows):
    per = tm // rows
    last = n_rows // rows - 1
    prev_map = lambda i: jnp.maximum(i * per - 1, 0)
    next_map = lambda i: jnp.minimum((i + 1) * per, last)
    return prev_map, next_map


def _dwconv3(v, prev_row, next_row, w, b):
    tm = v.shape[0]
    row = lax.broadcasted_iota(jnp.int32, v.shape, 0)
    v_prev = jnp.where(row == 0, prev_row, pltpu.roll(v, 1, 0))
    v_next = jnp.where(row == tm - 1, next_row, pltpu.roll(v, tm - 1, 0))
    return v_prev * w[0:1, :] + v * w[1:2, :] + v_next * w[2:3, :] + b


def _shortconv_out_kernel(z_ref, zp_ref, zn_ref, x_ref, cw_ref, cb_ref, w_ref, o_ref,
                          *, tiles_per_seq):
    i = pl.program_id(0)
    D = x_ref.shape[1]
    first = (i % tiles_per_seq) == 0
    last = ((i + 1) % tiles_per_seq) == 0

    def gated(ref):
        return ref[:, 2 * D:3 * D].astype(F32) * ref[:, 0:D].astype(F32)

    v = gated(z_ref)
    vp = jnp.where(first, 0.0, gated(zp_ref)[BF16_ROWS - 1:BF16_ROWS, :])
    vn = jnp.where(last, 0.0, gated(zn_ref)[0:1, :])
    y = z_ref[:, D:2 * D].astype(F32) * _dwconv3(v, vp, vn, cw_ref[...], cb_ref[...])
    o_ref[...] = x_ref[...] + jnp.dot(y.astype(BF16), w_ref[...],
                                      preferred_element_type=F32)


def _shortconv_out(z, x2d, conv_w, conv_b, w_out, *, seq_len, tm):
    T, D = x2d.shape
    assert seq_len % tm == 0
    prev_map, next_map = _halo_maps(tm, T, BF16_ROWS)
    kern = functools.partial(_shortconv_out_kernel, tiles_per_seq=seq_len // tm)
    return pl.pallas_call(
        kern,
        grid=(T // tm,),
        in_specs=[
            pl.BlockSpec((tm, 3 * D), lambda i: (i, 0)),
            pl.BlockSpec((BF16_ROWS, 3 * D), lambda i: (prev_map(i), 0)),
            pl.BlockSpec((BF16_ROWS, 3 * D), lambda i: (next_map(i), 0)),
            pl.BlockSpec((tm, D), lambda i: (i, 0)),
            pl.BlockSpec((3, D), lambda i: (0, 0)),
            pl.BlockSpec((1, D), lambda i: (0, 0)),
            pl.BlockSpec((D, D), lambda i: (0, 0)),
        ],
        out_specs=pl.BlockSpec((tm, D), lambda i: (i, 0)),
        out_shape=jax.ShapeDtypeStruct((T, D), F32),
        compiler_params=_params(("parallel",)),
        name="shortconv_out",
    )(z, z, z, x2d, conv_w, conv_b.reshape(1, D), w_out)


def _norm_up_gate_kernel(x_ref, xp_ref, xn_ref, g_ref, wa_ref, wb_ref, cwa_ref, cwb_ref,
                         cba_ref, cbb_ref, o_ref, h_ref, ra_ref, rb_ref, *, tiles_per_seq):
    i = pl.program_id(0)
    tm = x_ref.shape[0]
    halo = BF16_ROWS

    @pl.when(pl.program_id(1) == 0)
    def _():
        first = (i % tiles_per_seq) == 0
        last = ((i + 1) % tiles_per_seq) == 0
        gain = g_ref[...]
        h_ref[0:halo, :] = jnp.where(first, 0.0, _rmsnorm(xp_ref[...], gain)).astype(BF16)
        h_ref[halo:halo + tm, :] = _rmsnorm(x_ref[...], gain).astype(BF16)
        h_ref[halo + tm:, :] = jnp.where(last, 0.0, _rmsnorm(xn_ref[...], gain)).astype(BF16)

    def conv(r_ref, w_ref, cw_ref, cb_ref):
        r_ref[...] = jnp.dot(h_ref[...], w_ref[...].astype(BF16), preferred_element_type=F32)
        return (r_ref[halo - 1:halo - 1 + tm, :] * cw_ref[0:1, :]
                + r_ref[halo:halo + tm, :] * cw_ref[1:2, :]
                + r_ref[halo + 1:halo + 1 + tm, :] * cw_ref[2:3, :] + cb_ref[...])

    a = conv(ra_ref, wa_ref, cwa_ref, cba_ref)
    b = conv(rb_ref, wb_ref, cwb_ref, cbb_ref)
    o_ref[...] = ((a / (1.0 + jnp.exp(-a))) * b).astype(o_ref.dtype)


def _norm_up_gate(x2d, gain, w_up_all, layer, conv_w, conv_b, *, seq_len, tm, tn):
    T, D = x2d.shape
    F = w_up_all.shape[2] // 2
    assert seq_len % tm == 0 and F % tn == 0
    nj = F // tn
    prev_map, next_map = _halo_maps(tm, T, BF16_ROWS)
    cb = conv_b.reshape(1, 2 * F)

    def chan(rows, off):
        return pl.BlockSpec((rows, tn), lambda i, j: (0, off + j))

    def weight(off):
        return pl.BlockSpec((None, D, tn), lambda i, j: (layer, 0, off + j))

    return pl.pallas_call(
        functools.partial(_norm_up_gate_kernel, tiles_per_seq=seq_len // tm),
        grid=(T // tm, nj),
        in_specs=[
            pl.BlockSpec((tm, D), lambda i, j: (i, 0)),
            pl.BlockSpec((BF16_ROWS, D), lambda i, j: (prev_map(i), 0)),
            pl.BlockSpec((BF16_ROWS, D), lambda i, j: (next_map(i), 0)),
            pl.BlockSpec((1, D), lambda i, j: (0, 0)),
            weight(0), weight(nj), chan(3, 0), chan(3, nj), chan(1, 0), chan(1, nj),
        ],
        out_specs=pl.BlockSpec((tm, tn), lambda i, j: (i, j)),
        out_shape=jax.ShapeDtypeStruct((T, F), BF16),
        scratch_shapes=[pltpu.VMEM((tm + 2 * BF16_ROWS, D), BF16),
                        pltpu.VMEM((tm + 2 * BF16_ROWS, tn), F32),
                        pltpu.VMEM((tm + 2 * BF16_ROWS, tn), F32)],
        compiler_params=_params(("parallel", "arbitrary")),
        name="norm_up_gate",
    )(x2d, x2d, x2d, gain.reshape(1, D), w_up_all, w_up_all, conv_w, conv_w, cb, cb)


def _down_residual_kernel(g_ref, w_ref, x_ref, gain_ref, o_ref, *, final_norm):
    y = x_ref[...] + jnp.dot(g_ref[...], w_ref[...], preferred_element_type=F32)
    o_ref[...] = _rmsnorm(y, gain_ref[...]) if final_norm else y


def _down_residual(g, w_down, x2d, final_gain, *, tm):
    T, D = x2d.shape
    F = w_down.shape[0]
    assert T % tm == 0
    final_norm = final_gain is not None
    gain = final_gain if final_norm else jnp.ones((D,), F32)
    return pl.pallas_call(
        functools.partial(_down_residual_kernel, final_norm=final_norm),
        grid=(T // tm,),
        in_specs=[
            pl.BlockSpec((tm, F), lambda i: (i, 0)),
            pl.BlockSpec((F, D), lambda i: (0, 0), pipeline_mode=pl.Buffered(1)),
            pl.BlockSpec((tm, D), lambda i: (i, 0)),
            pl.BlockSpec((1, D), lambda i: (0, 0)),
        ],
        out_specs=pl.BlockSpec((tm, D), lambda i: (i, 0)),
        out_shape=jax.ShapeDtypeStruct((T, D), F32),
        compiler_params=_params(("parallel",)),
        name="down_residual",
    )(g, w_down, x2d, gain.reshape(1, D))


def _banded_attention_kernel(slope_ref, q_ref, k_ref, kp_ref, kn_ref, v_ref, vp_ref, vn_ref,
                             acc_ref, m_ref, lse_ref, kbuf, vbuf, bias_ref, slab_ref,
                             mstat_ref, lstat_ref, *, dil, seq, half, qs):
    lq = q_ref.shape[1]
    dh = q_ref.shape[2] // ATTN_HEADS
    n_sb = lq // qs
    it, hg = pl.program_id(1), pl.program_id(2)

    @pl.when(hg == 0)
    def _():
        mstat_ref[...] = jnp.zeros(mstat_ref.shape, F32)
        lstat_ref[...] = jnp.zeros(lstat_ref.shape, F32)

    for buf, prev, main, nxt in ((kbuf, kp_ref, k_ref, kn_ref), (vbuf, vp_ref, v_ref, vn_ref)):
        buf[:, 0:half, :] = prev[...]
        buf[:, half:half + lq, :] = main[...]
        buf[:, half + lq:, :] = nxt[...]

    win = qs + 2 * half
    ii = lax.broadcasted_iota(jnp.int32, (qs, win), 0)
    jj = lax.broadcasted_iota(jnp.int32, (qs, win), 1)
    delta = jnp.abs(jj - half - ii)
    dist = (dil * delta).astype(F32)
    lane = lax.broadcasted_iota(jnp.int32, (qs, LANES), 1)
    head0 = hg * ATTN_HEADS
    own_lanes = (lane >= head0) & (lane < head0 + ATTN_HEADS)
    scale2 = dh ** -0.5 * LOG2E
    neg_slopes2 = [-slope_ref[head0 + hh] * LOG2E for hh in range(ATTN_HEADS)]

    variants = sorted({(sb == 0, sb == n_sb - 1) for sb in range(n_sb)})
    for vi, (is_first, is_last) in enumerate(variants):
        masked = delta > half
        if is_first:
            masked = masked | ((it == 0) & (jj < half))
        if is_last:
            masked = masked | ((it == seq // lq - 1) & (jj >= qs + half))
        for hh in range(ATTN_HEADS):
            bias_ref[vi * ATTN_HEADS + hh] = jnp.where(masked, NEG_INF, neg_slopes2[hh] * dist)

    for sb in range(n_sb):
        vi = variants.index((sb == 0, sb == n_sb - 1))
        for r in range(dil):
            rows = pl.ds(sb * qs, qs) if dil == 1 else pl.ds(r + sb * qs * dil, qs, stride=dil)
            m_tile = jnp.zeros((qs, LANES), F32)
            den_tile = jnp.ones((qs, LANES), F32)
            for hh in range(ATTN_HEADS):
                cols = slice(hh * dh, (hh + 1) * dh)
                q = q_ref[r, sb * qs:(sb + 1) * qs, cols]
                kw = kbuf[r, sb * qs:sb * qs + win, cols]
                vw = vbuf[r, sb * qs:sb * qs + win, cols]
                t = lax.dot_general(q, kw, (((1,), (1,)), ((), ())),
                                    preferred_element_type=F32) * scale2
                t = t + bias_ref[vi * ATTN_HEADS + hh]
                m = jnp.max(t, axis=-1, keepdims=True)
                p = jnp.exp2(t - m)
                den = jnp.sum(p, axis=-1, keepdims=True)
                slab_ref[hh, rows, :] = jnp.dot(p.astype(BF16), vw, preferred_element_type=F32)
                m_tile = jnp.where(lane == head0 + hh, m, m_tile)
                den_tile = jnp.where(lane == head0 + hh, den, den_tile)
            srows = pl.ds(r * lq + sb * qs, qs)
            mstat_ref[srows, :] = jnp.where(own_lanes, m_tile, mstat_ref[srows, :])
            lstat_ref[srows, :] = jnp.where(own_lanes, m_tile + jnp.log2(den_tile),
                                            lstat_ref[srows, :])

    for hh in range(ATTN_HEADS):
        acc_ref[:, hh * dh:(hh + 1) * dh] = slab_ref[hh].astype(acc_ref.dtype)

    @pl.when(hg == pl.num_programs(2) - 1)
    def _():
        for r in range(dil):
            rows = pl.ds(0, lq) if dil == 1 else pl.ds(r, lq, stride=dil)
            m_ref[rows, :] = mstat_ref[r * lq:(r + 1) * lq, :]
            lse_ref[rows, :] = lstat_ref[r * lq:(r + 1) * lq, :]


def _banded_attention(qkv, slopes, *, window, dil):
    B, d, L, D3 = qkv.shape
    D = D3 // 3
    dh = D // N_HEADS
    half = (window // 2) // dil
    qs = LANES
    lq = max(ATTN_TOKENS // dil, qs)
    tokens = lq * dil
    wcols = ATTN_HEADS * dh
    n_hg = N_HEADS // ATTN_HEADS
    assert d == dil and L % lq == 0 and lq % half == 0 and half % BF16_ROWS == 0
    n_t = L // lq
    per = lq // half
    last = L // half - 1

    def main(c):
        return pl.BlockSpec((None, dil, lq, wcols), lambda b, i, g: (b, 0, i, c * n_hg + g))

    def prev(c):
        return pl.BlockSpec((None, dil, half, wcols),
                            lambda b, i, g: (b, 0, jnp.maximum(i * per - 1, 0), c * n_hg + g))

    def nxt(c):
        return pl.BlockSpec((None, dil, half, wcols),
                            lambda b, i, g: (b, 0, jnp.minimum((i + 1) * per, last), c * n_hg + g))

    kern = functools.partial(_banded_attention_kernel, dil=dil, seq=L, half=half, qs=qs)
    T = B * L * dil
    return pl.pallas_call(
        kern,
        grid=(B, n_t, n_hg),
        in_specs=[pl.BlockSpec(memory_space=pltpu.SMEM),
                  main(0), main(1), prev(1), nxt(1), main(2), prev(2), nxt(2)],
        out_specs=[pl.BlockSpec((tokens, wcols), lambda b, i, g: (b * n_t + i, g)),
                   pl.BlockSpec((tokens, LANES), lambda b, i, g: (b * n_t + i, 0)),
                   pl.BlockSpec((tokens, LANES), lambda b, i, g: (b * n_t + i, 0))],
        out_shape=[jax.ShapeDtypeStruct((T, D), BF16),
                   jax.ShapeDtypeStruct((T, LANES), F32),
                   jax.ShapeDtypeStruct((T, LANES), F32)],
        scratch_shapes=[pltpu.VMEM((dil, lq + 2 * half, wcols), BF16),
                        pltpu.VMEM((dil, lq + 2 * half, wcols), BF16),
                        pltpu.VMEM((min(lq // qs, 3) * ATTN_HEADS, qs, qs + 2 * half), F32),
                        pltpu.VMEM((ATTN_HEADS, tokens, dh), F32),
                        pltpu.VMEM((tokens, LANES), F32),
                        pltpu.VMEM((tokens, LANES), F32)],
        compiler_params=_params(("parallel", "parallel", "arbitrary")),
        name="banded_attention",
    )(slopes, qkv, qkv, qkv, qkv, qkv, qkv, qkv)


def _combine_out_kernel(*refs, n_groups):
    acc_refs = refs[:n_groups]
    m_refs = refs[n_groups:2 * n_groups]
    l_refs = refs[2 * n_groups:3 * n_groups]
    x_ref, w_ref, out_ref, mix_ref = refs[3 * n_groups:]
    dh = x_ref.shape[1] // N_HEADS
    lses = [r[...] for r in l_refs]
    top = functools.reduce(jnp.maximum, lses)
    total = functools.reduce(lambda a, b: a + b, [jnp.exp2(l - top) for l in lses])
    lse_all = top + jnp.log2(total)
    coefs = [jnp.exp2(r[...] - lse_all) for r in m_refs]
    for h in range(N_HEADS):
        cols = slice(h * dh, (h + 1) * dh)
        mixed = functools.reduce(
            lambda a, b: a + b,
            [c[:, h:h + 1] * acc[:, cols].astype(F32) for c, acc in zip(coefs, acc_refs)])
        mix_ref[:, cols] = mixed.astype(mix_ref.dtype)
    out_ref[...] = x_ref[...] + jnp.dot(mix_ref[...], w_ref[...],
                                        preferred_element_type=F32)


def _combine_out(accs, maxes, lses, x2d, w_out, *, tm):
    T, D = x2d.shape
    n = len(accs)
    return pl.pallas_call(
        functools.partial(_combine_out_kernel, n_groups=n),
        grid=(T // tm,),
        in_specs=([pl.BlockSpec((tm, D), lambda i: (i, 0))] * n
                  + [pl.BlockSpec((tm, LANES), lambda i: (i, 0))] * (2 * n)
                  + [pl.BlockSpec((tm, D), lambda i: (i, 0)),
                     pl.BlockSpec((D, D), lambda i: (0, 0))]),
        out_specs=pl.BlockSpec((tm, D), lambda i: (i, 0)),
        out_shape=jax.ShapeDtypeStruct((T, D), F32),
        scratch_shapes=[pltpu.VMEM((tm, D), BF16)],
        compiler_params=_params(("parallel",)),
        name="combine_out",
    )(*accs, *maxes, *lses, x2d, w_out)


def kernel(x, mix_norm_g, ffn_norm_g, final_norm_g, sc_w_in, sc_conv_w, sc_conv_b, sc_w_out,
           attn_w_qkv, attn_w_out, ffn_w_up, ffn_conv_w, ffn_conv_b, ffn_w_down):
    B, S, D = x.shape
    T = B * S
    depth = mix_norm_g.shape[0]
    n_mixers = 2
    slopes = jnp.asarray(2.0 ** (-ALIBI_MAX * np.arange(1, N_HEADS + 1) / N_HEADS), dtype=F32)
    xf = x.reshape(T, D)
    for i in range(depth):
        j = i // n_mixers
        if i % n_mixers == 0:
            z = _norm_matmul(xf, mix_norm_g[i], sc_w_in, j, batch=B, dil=1,
                             col0=0, n_cols=3 * D, tm=1024, tn=1024).reshape(T, 3 * D)
            xf = _shortconv_out(z, xf, sc_conv_w[j], sc_conv_b[j], sc_w_out[j].astype(BF16),
                                seq_len=S, tm=256)
        else:
            parts = []
            for g, (window, dil) in enumerate(DILATED_GROUPS):
                qkv = _norm_matmul(xf, mix_norm_g[i], attn_w_qkv, j, batch=B, dil=dil,
                                   col0=g * 3 * D, n_cols=3 * D, tm=1024, tn=1024)
                parts.append(_banded_attention(qkv, slopes, window=window, dil=dil))
            accs, maxes, lses = zip(*parts)
            xf = _combine_out(accs, maxes, lses, xf, attn_w_out[j].astype(BF16), tm=256)
        g = _norm_up_gate(xf, ffn_norm_g[i], ffn_w_up, i, ffn_conv_w[i],
                          ffn_conv_b[i], seq_len=S, tm=1024, tn=512)
        xf = _down_residual(g, ffn_w_down[i].astype(BF16), xf,
                            final_norm_g if i == depth - 1 else None, tm=512)
    return xf.reshape(B, S, D)
```

```python
import functools

import numpy as np
import jax
import jax.numpy as jnp
from jax import lax
from jax.experimental import pallas as pl
from jax.experimental.pallas import tpu as pltpu

N_HEADS = 16
DILATED_GROUPS = ((128, 1), (512, 4), (2048, 16))
NORM_EPS = 1e-5
ALIBI_MAX = 8.0
NEG_INF = -1e30
LOG2E = 1.4426950408889634

LANES = 128
BF16_ROWS = 16
MXU_DIM = 256
VMEM_LIMIT = 56 * 1024 * 1024
ATTN_TOKENS = 1024
ATTN_HEADS = 4

F32 = jnp.float32
BF16 = jnp.bfloat16


def _params(semantics):
    return pltpu.CompilerParams(dimension_semantics=semantics,
                                vmem_limit_bytes=VMEM_LIMIT)


def _rmsnorm(x, gain):
    ms = jnp.mean(x * x, axis=-1, keepdims=True)
    return x * lax.rsqrt(ms + NORM_EPS) * gain


def _log2(n):
    assert n & (n - 1) == 0
    return n.bit_length() - 1


def _norm_matmul_kernel(x_ref, g_ref, w_ref, o_ref, h_ref, *, dil):
    tm = x_ref.shape[0]
    per = tm // dil

    @pl.when(pl.program_id(1) == 0)
    def _():
        h = _rmsnorm(x_ref[...], g_ref[...]).astype(BF16)
        if dil == 1:
            h_ref[...] = h
        else:
            n = MXU_DIM // dil
            row = lax.broadcasted_iota(jnp.int32, (MXU_DIM, MXU_DIM), 0)
            col = lax.broadcasted_iota(jnp.int32, (MXU_DIM, MXU_DIM), 1)
            src = (row & (n - 1)) * dil + lax.shift_right_logical(row, _log2(n))
            perm = jnp.where(col == src, 1.0, 0.0).astype(BF16)
            for c in range(tm // MXU_DIM):
                hc = jnp.dot(perm, h[c * MXU_DIM:(c + 1) * MXU_DIM, :],
                             preferred_element_type=F32).astype(BF16)
                for r in range(dil):
                    h_ref[r * per + c * n:r * per + (c + 1) * n, :] = hc[r * n:(r + 1) * n, :]

    res = jnp.dot(h_ref[...], w_ref[...].astype(BF16),
                  preferred_element_type=F32).astype(o_ref.dtype)
    for r in range(dil):
        o_ref[r] = res[r * per:(r + 1) * per, :]


def _norm_matmul(x2d, gain, w_all, layer, *, batch, dil, col0, n_cols, tm, tn):
    T, D = x2d.shape
    S = T // batch
    assert S % tm == 0 and n_cols % tn == 0 and col0 % tn == 0
    assert tm % MXU_DIM == 0 and (MXU_DIM // dil) % BF16_ROWS == 0
    tiles = S // tm
    return pl.pallas_call(
        functools.partial(_norm_matmul_kernel, dil=dil),
        grid=(T // tm, n_cols // tn),
        in_specs=[
            pl.BlockSpec((tm, D), lambda m, j: (m, 0)),
            pl.BlockSpec((1, D), lambda m, j: (0, 0)),
            pl.BlockSpec((None, D, tn), lambda m, j: (layer, 0, col0 // tn + j)),
        ],
        out_specs=pl.BlockSpec((None, dil, tm // dil, tn),
                               lambda m, j: (m // tiles, 0, m % tiles, j)),
        out_shape=jax.ShapeDtypeStruct((batch, dil, S // dil, n_cols), BF16),
        scratch_shapes=[pltpu.VMEM((tm, D), BF16)],
        compiler_params=_params(("parallel", "arbitrary")),
        name="norm_matmul",
    )(x2d, gain.reshape(1, D), w_all)


def _halo_maps(tm, n_rows, rows):
    per = tm // rows
    last = n_rows // rows - 1
    prev_map = lambda i: jnp.maximum(i * per - 1, 0)
    next_map = lambda i: jnp.minimum((i + 1) * per, last)
    return prev_map, next_map


def _dwconv3(v, prev_row, next_row, w, b):
    tm = v.shape[0]
    row = lax.broadcasted_iota(jnp.int32, v.shape, 0)
    v_prev = jnp.where(row == 0, prev_row, pltpu.roll(v, 1, 0))
    v_next = jnp.where(row == tm - 1, next_row, pltpu.roll(v, tm - 1, 0))
    return v_prev * w[0:1, :] + v * w[1:2, :] + v_next * w[2:3, :] + b


def _shortconv_out_kernel(z_ref, zp_ref, zn_ref, x_ref, cw_ref, cb_ref, w_ref, o_ref,
                          *, tiles_per_seq):
    i = pl.program_id(0)
    D = x_ref.shape[1]
    first = (i % tiles_per_seq) == 0
    last = ((i + 1) % tiles_per_seq) == 0

    def gated(ref):
        return ref[:, 2 * D:3 * D].astype(F32) * ref[:, 0:D].astype(F32)

    v = gated(z_ref)
    vp = jnp.where(first, 0.0, gated(zp_ref)[BF16_ROWS - 1:BF16_ROWS, :])
    vn = jnp.where(last, 0.0, gated(zn_ref)[0:1, :])
    y = z_ref[:, D:2 * D].astype(F32) * _dwconv3(v, vp, vn, cw_ref[...], cb_ref[...])
    o_ref[...] = x_ref[...] + jnp.dot(y.astype(BF16), w_ref[...],
                                      preferred_element_type=F32)


def _shortconv_out(z, x2d, conv_w, conv_b, w_out, *, seq_len, tm):
    T, D = x2d.shape
    assert seq_len % tm == 0
    prev_map, next_map = _halo_maps(tm, T, BF16_ROWS)
    kern = functools.partial(_shortconv_out_kernel, tiles_per_seq=seq_len // tm)
    return pl.pallas_call(
        kern,
        grid=(T // tm,),
        in_specs=[
            pl.BlockSpec((tm, 3 * D), lambda i: (i, 0)),
            pl.BlockSpec((BF16_ROWS, 3 * D), lambda i: (prev_map(i), 0)),
            pl.BlockSpec((BF16_ROWS, 3 * D), lambda i: (next_map(i), 0)),
            pl.BlockSpec((tm, D), lambda i: (i, 0)),
            pl.BlockSpec((3, D), lambda i: (0, 0)),
            pl.BlockSpec((1, D), lambda i: (0, 0)),
            pl.BlockSpec((D, D), lambda i: (0, 0)),
        ],
        out_specs=pl.BlockSpec((tm, D), lambda i: (i, 0)),
        out_shape=jax.ShapeDtypeStruct((T, D), F32),
        compiler_params=_params(("parallel",)),
        name="shortconv_out",
    )(z, z, z, x2d, conv_w, conv_b.reshape(1, D), w_out)


def _norm_up_gate_kernel(x_ref, xp_ref, xn_ref, g_ref, wa_ref, wb_ref, cwa_ref, cwb_ref,
                         cba_ref, cbb_ref, o_ref, h_ref, *, tiles_per_seq):
    i = pl.program_id(0)
    tm = x_ref.shape[0]
    halo = BF16_ROWS
    rows = tm + 2 * halo

    @pl.when(pl.program_id(1) == 0)
    def _():
        first = (i % tiles_per_seq) == 0
        last = ((i + 1) % tiles_per_seq) == 0
        gain = g_ref[...]
        h_ref[0:halo, :] = jnp.where(first, 0.0, _rmsnorm(xp_ref[...], gain)).astype(BF16)
        h_ref[halo:halo + tm, :] = _rmsnorm(x_ref[...], gain).astype(BF16)
        h_ref[halo + tm:, :] = jnp.where(last, 0.0, _rmsnorm(xn_ref[...], gain)).astype(BF16)

    def conv(w_ref, cw_ref, cb_ref):
        r = jnp.dot(h_ref[...], w_ref[...].astype(BF16), preferred_element_type=F32)
        return (pltpu.roll(r, 1, 0) * cw_ref[0:1, :] + r * cw_ref[1:2, :]
                + pltpu.roll(r, rows - 1, 0) * cw_ref[2:3, :] + cb_ref[...])

    a = conv(wa_ref, cwa_ref, cba_ref)
    b = conv(wb_ref, cwb_ref, cbb_ref)
    gated = (a / (1.0 + jnp.exp(-a))) * b
    o_ref[...] = gated[halo:halo + tm, :].astype(o_ref.dtype)


def _norm_up_gate(x2d, gain, w_up_all, layer, conv_w, conv_b, *, seq_len, tm, tn):
    T, D = x2d.shape
    F = w_up_all.shape[2] // 2
    assert seq_len % tm == 0 and F % tn == 0
    nj = F // tn
    prev_map, next_map = _halo_maps(tm, T, BF16_ROWS)
    cb = conv_b.reshape(1, 2 * F)

    def chan(rows, off):
        return pl.BlockSpec((rows, tn), lambda i, j: (0, off + j))

    def weight(off):
        return pl.BlockSpec((None, D, tn), lambda i, j: (layer, 0, off + j))

    return pl.pallas_call(
        functools.partial(_norm_up_gate_kernel, tiles_per_seq=seq_len // tm),
        grid=(T // tm, nj),
        in_specs=[
            pl.BlockSpec((tm, D), lambda i, j: (i, 0)),
            pl.BlockSpec((BF16_ROWS, D), lambda i, j: (prev_map(i), 0)),
            pl.BlockSpec((BF16_ROWS, D), lambda i, j: (next_map(i), 0)),
            pl.BlockSpec((1, D), lambda i, j: (0, 0)),
            weight(0), weight(nj), chan(3, 0), chan(3, nj), chan(1, 0), chan(1, nj),
        ],
        out_specs=pl.BlockSpec((tm, tn), lambda i, j: (i, j)),
        out_shape=jax.ShapeDtypeStruct((T, F), BF16),
        scratch_shapes=[pltpu.VMEM((tm + 2 * BF16_ROWS, D), BF16)],
        compiler_params=_params(("parallel", "arbitrary")),
        name="norm_up_gate",
    )(x2d, x2d, x2d, gain.reshape(1, D), w_up_all, w_up_all, conv_w, conv_w, cb, cb)


def _down_residual_kernel(g_ref, w_ref, x_ref, gain_ref, o_ref, *, final_norm):
    y = x_ref[...] + jnp.dot(g_ref[...], w_ref[...], preferred_element_type=F32)
    o_ref[...] = _rmsnorm(y, gain_ref[...]) if final_norm else y


def _down_residual(g, w_down, x2d, final_gain, *, tm):
    T, D = x2d.shape
    F = w_down.shape[0]
    assert T % tm == 0
    final_norm = final_gain is not None
    gain = final_gain if final_norm else jnp.ones((D,), F32)
    return pl.pallas_call(
        functools.partial(_down_residual_kernel, final_norm=final_norm),
        grid=(T // tm,),
        in_specs=[
            pl.BlockSpec((tm, F), lambda i: (i, 0)),
            pl.BlockSpec((F, D), lambda i: (0, 0), pipeline_mode=pl.Buffered(1)),
            pl.BlockSpec((tm, D), lambda i: (i, 0)),
            pl.BlockSpec((1, D), lambda i: (0, 0)),
        ],
        out_specs=pl.BlockSpec((tm, D), lambda i: (i, 0)),
        out_shape=jax.ShapeDtypeStruct((T, D), F32),
        compiler_params=_params(("parallel",)),
        name="down_residual",
    )(g, w_down, x2d, gain.reshape(1, D))


def _banded_attention_kernel(slope_ref, q_ref, k_ref, kp_ref, kn_ref, v_ref, vp_ref, vn_ref,
                             acc_ref, m_ref, lse_ref, kbuf, vbuf, bias_ref, slab_ref,
                             mstat_ref, lstat_ref, *, dil, seq, half, qs):
    lq = q_ref.shape[1]
    dh = q_ref.shape[2] // ATTN_HEADS
    n_sb = lq // qs
    it, hg = pl.program_id(1), pl.program_id(2)

    @pl.when(hg == 0)
    def _():
        mstat_ref[...] = jnp.zeros(mstat_ref.shape, F32)
        lstat_ref[...] = jnp.zeros(lstat_ref.shape, F32)

    for buf, prev, main, nxt in ((kbuf, kp_ref, k_ref, kn_ref), (vbuf, vp_ref, v_ref, vn_ref)):
        buf[:, 0:half, :] = prev[...]
        buf[:, half:half + lq, :] = main[...]
        buf[:, half + lq:, :] = nxt[...]

    win = qs + 2 * half
    ii = lax.broadcasted_iota(jnp.int32, (qs, win), 0)
    jj = lax.broadcasted_iota(jnp.int32, (qs, win), 1)
    delta = jnp.abs(jj - half - ii)
    dist = (dil * delta).astype(F32)
    lane = lax.broadcasted_iota(jnp.int32, (qs, LANES), 1)
    head0 = hg * ATTN_HEADS
    own_lanes = (lane >= head0) & (lane < head0 + ATTN_HEADS)
    scale2 = dh ** -0.5 * LOG2E
    neg_slopes2 = [-slope_ref[head0 + hh] * LOG2E for hh in range(ATTN_HEADS)]

    variants = sorted({(sb == 0, sb == n_sb - 1) for sb in range(n_sb)})
    for vi, (is_first, is_last) in enumerate(variants):
        masked = delta > half
        if is_first:
            masked = masked | ((it == 0) & (jj < half))
        if is_last:
            masked = masked | ((it == seq // lq - 1) & (jj >= qs + half))
        for hh in range(ATTN_HEADS):
            bias_ref[vi * ATTN_HEADS + hh] = jnp.where(masked, NEG_INF, neg_slopes2[hh] * dist)

    for sb in range(n_sb):
        vi = variants.index((sb == 0, sb == n_sb - 1))
        for r in range(dil):
            rows = pl.ds(sb * qs, qs) if dil == 1 else pl.ds(r + sb * qs * dil, qs, stride=dil)
            m_tile = jnp.zeros((qs, LANES), F32)
            den_tile = jnp.ones((qs, LANES), F32)
            for hh in range(ATTN_HEADS):
                cols = slice(hh * dh, (hh + 1) * dh)
                q = q_ref[r, sb * qs:(sb + 1) * qs, cols]
                kw = kbuf[r, sb * qs:sb * qs + win, cols]
                vw = vbuf[r, sb * qs:sb * qs + win, cols]
                t = lax.dot_general(q, kw, (((1,), (1,)), ((), ())),
                                    preferred_element_type=F32) * scale2
                t = t + bias_ref[vi * ATTN_HEADS + hh]
                m = jnp.max(t, axis=-1, keepdims=True)
                p = jnp.exp2(t - m)
                den = jnp.sum(p, axis=-1, keepdims=True)
                slab_ref[hh, rows, :] = jnp.dot(p.astype(BF16), vw, preferred_element_type=F32)
                m_tile = jnp.where(lane == head0 + hh, m, m_tile)
                den_tile = jnp.where(lane == head0 + hh, den, den_tile)
            srows = pl.ds(r * lq + sb * qs, qs)
            mstat_ref[srows, :] = jnp.where(own_lanes, m_tile, mstat_ref[srows, :])
            lstat_ref[srows, :] = jnp.where(own_lanes, m_tile + jnp.log2(den_tile),
                                            lstat_ref[srows, :])

    for hh in range(ATTN_HEADS):
        acc_ref[:, hh * dh:(hh + 1) * dh] = slab_ref[hh].astype(acc_ref.dtype)

    @pl.when(hg == pl.num_programs(2) - 1)
    def _():
        for r in range(dil):
            rows = pl.ds(0, lq) if dil == 1 else pl.ds(r, lq, stride=dil)
            m_ref[rows, :] = mstat_ref[r * lq:(r + 1) * lq, :]
            lse_ref[rows, :] = lstat_ref[r * lq:(r + 1) * lq, :]


def _banded_attention(qkv, slopes, *, window, dil):
    B, d, L, D3 = qkv.shape
    D = D3 // 3
    dh = D // N_HEADS
    half = (window // 2) // dil
    qs = LANES
    lq = max(ATTN_TOKENS // dil, qs)
    tokens = lq * dil
    wcols = ATTN_HEADS * dh
    n_hg = N_HEADS // ATTN_HEADS
    assert d == dil and L % lq == 0 and lq % half == 0 and half % BF16_ROWS == 0
    n_t = L // lq
    per = lq // half
    last = L // half - 1

    def main(c):
        return pl.BlockSpec((None, dil, lq, wcols), lambda b, i, g: (b, 0, i, c * n_hg + g))

    def prev(c):
        return pl.BlockSpec((None, dil, half, wcols),
                            lambda b, i, g: (b, 0, jnp.maximum(i * per - 1, 0), c * n_hg + g))

    def nxt(c):
        return pl.BlockSpec((None, dil, half, wcols),
                            lambda b, i, g: (b, 0, jnp.minimum((i + 1) * per, last), c * n_hg + g))

    kern = functools.partial(_banded_attention_kernel, dil=dil, seq=L, half=half, qs=qs)
    T = B * L * dil
    return pl.pallas_call(
        kern,
        grid=(B, n_t, n_hg),
        in_specs=[pl.BlockSpec(memory_space=pltpu.SMEM),
                  main(0), main(1), prev(1), nxt(1), main(2), prev(2), nxt(2)],
        out_specs=[pl.BlockSpec((tokens, wcols), lambda b, i, g: (b * n_t + i, g)),
                   pl.BlockSpec((tokens, LANES), lambda b, i, g: (b * n_t + i, 0)),
                   pl.BlockSpec((tokens, LANES), lambda b, i, g: (b * n_t + i, 0))],
        out_shape=[jax.ShapeDtypeStruct((T, D), BF16),
                   jax.ShapeDtypeStruct((T, LANES), F32),
                   jax.ShapeDtypeStruct((T, LANES), F32)],
        scratch_shapes=[pltpu.VMEM((dil, lq + 2 * half, wcols), BF16),
                        pltpu.VMEM((dil, lq + 2 * half, wcols), BF16),
                        pltpu.VMEM((min(lq // qs, 3) * ATTN_HEADS, qs, qs + 2 * half), F32),
                        pltpu.VMEM((ATTN_HEADS, tokens, dh), F32),
                        pltpu.VMEM((tokens, LANES), F32),
                        pltpu.VMEM((tokens, LANES), F32)],
        compiler_params=_params(("parallel", "parallel", "arbitrary")),
        name="banded_attention",
    )(slopes, qkv, qkv, qkv, qkv, qkv, qkv, qkv)


def _combine_out_kernel(*refs, n_groups):
    acc_refs = refs[:n_groups]
    m_refs = refs[n_groups:2 * n_groups]
    l_refs = refs[2 * n_groups:3 * n_groups]
    x_ref, w_ref, out_ref, mix_ref = refs[3 * n_groups:]
    dh = x_ref.shape[1] // N_HEADS
    lses = [r[...] for r in l_refs]
    top = functools.reduce(jnp.maximum, lses)
    total = functools.reduce(lambda a, b: a + b, [jnp.exp2(l - top) for l in lses])
    lse_all = top + jnp.log2(total)
    coefs = [jnp.exp2(r[...] - lse_all) for r in m_refs]
    for h in range(N_HEADS):
        cols = slice(h * dh, (h + 1) * dh)
        mixed = functools.reduce(
            lambda a, b: a + b,
            [c[:, h:h + 1] * acc[:, cols].astype(F32) for c, acc in zip(coefs, acc_refs)])
        mix_ref[:, cols] = mixed.astype(mix_ref.dtype)
    out_ref[...] = x_ref[...] + jnp.dot(mix_ref[...], w_ref[...],
                                        preferred_element_type=F32)


def _combine_out(accs, maxes, lses, x2d, w_out, *, tm):
    T, D = x2d.shape
    n = len(accs)
    return pl.pallas_call(
        functools.partial(_combine_out_kernel, n_groups=n),
        grid=(T // tm,),
        in_specs=([pl.BlockSpec((tm, D), lambda i: (i, 0))] * n
                  + [pl.BlockSpec((tm, LANES), lambda i: (i, 0))] * (2 * n)
                  + [pl.BlockSpec((tm, D), lambda i: (i, 0)),
                     pl.BlockSpec((D, D), lambda i: (0, 0))]),
        out_specs=pl.BlockSpec((tm, D), lambda i: (i, 0)),
        out_shape=jax.ShapeDtypeStruct((T, D), F32),
        scratch_shapes=[pltpu.VMEM((tm, D), BF16)],
        compiler_params=_params(("parallel",)),
        name="combine_out",
    )(*accs, *maxes, *lses, x2d, w_out)


def kernel(x, mix_norm_g, ffn_norm_g, final_norm_g, sc_w_in, sc_conv_w, sc_conv_b, sc_w_out,
           attn_w_qkv, attn_w_out, ffn_w_up, ffn_conv_w, ffn_conv_b, ffn_w_down):
    B, S, D = x.shape
    T = B * S
    depth = mix_norm_g.shape[0]
    n_mixers = 2
    slopes = jnp.asarray(2.0 ** (-ALIBI_MAX * np.arange(1, N_HEADS + 1) / N_HEADS), dtype=F32)
    xf = x.reshape(T, D)
    for i in range(depth):
        j = i // n_mixers
        if i % n_mixers == 0:
            z = _norm_matmul(xf, mix_norm_g[i], sc_w_in, j, batch=B, dil=1,
                             col0=0, n_cols=3 * D, tm=1024, tn=1024).reshape(T, 3 * D)
            xf = _shortconv_out(z, xf, sc_conv_w[j], sc_conv_b[j], sc_w_out[j].astype(BF16),
                                seq_len=S, tm=256)
        else:
            parts = []
            for g, (window, dil) in enumerate(DILATED_GROUPS):
                qkv = _norm_matmul(xf, mix_norm_g[i], attn_w_qkv, j, batch=B, dil=dil,
                                   col0=g * 3 * D, n_cols=3 * D, tm=1024, tn=1024)
                parts.append(_banded_attention(qkv, slopes, window=window, dil=dil))
            accs, maxes, lses = zip(*parts)
            xf = _combine_out(accs, maxes, lses, xf, attn_w_out[j].astype(BF16), tm=256)
        g = _norm_up_gate(xf, ffn_norm_g[i], ffn_w_up, i, ffn_conv_w[i],
                          ffn_conv_b[i], seq_len=S, tm=1024, tn=512)
        xf = _down_residual(g, ffn_w_down[i].astype(BF16), xf,
                            final_norm_g if i == depth - 1 else None, tm=512)
    return xf.reshape(B, S, D)
```

```python
import functools

import numpy as np
import jax
import jax.numpy as jnp
from jax import lax
from jax.experimental import pallas as pl
from jax.experimental.pallas import tpu as pltpu

N_HEADS = 16
DILATED_GROUPS = ((128, 1), (512, 4), (2048, 16))
NORM_EPS = 1e-5
ALIBI_MAX = 8.0
NEG_INF = -1e30
LOG2E = 1.4426950408889634

LANES = 128
BF16_ROWS = 16
MXU_DIM = 256
VMEM_LIMIT = 56 * 1024 * 1024
ATTN_TOKENS = 1024
ATTN_HEADS = 4

F32 = jnp.float32
BF16 = jnp.bfloat16


def _params(semantics):
    return pltpu.CompilerParams(dimension_semantics=semantics,
                                vmem_limit_bytes=VMEM_LIMIT)


def _rmsnorm(x, gain):
    ms = jnp.mean(x * x, axis=-1, keepdims=True)
    return x * lax.rsqrt(ms + NORM_EPS) * gain


def _log2(n):
    assert n & (n - 1) == 0
    return n.bit_length() - 1


def _norm_matmul_kernel(x_ref, g_ref, w_ref, o_ref, h_ref, *, dil):
    tm = x_ref.shape[0]
    per = tm // dil

    @pl.when(pl.program_id(1) == 0)
    def _():
        h = _rmsnorm(x_ref[...], g_ref[...]).astype(BF16)
        if dil == 1:
            h_ref[...] = h
        else:
            n = MXU_DIM // dil
            row = lax.broadcasted_iota(jnp.int32, (MXU_DIM, MXU_DIM), 0)
            col = lax.broadcasted_iota(jnp.int32, (MXU_DIM, MXU_DIM), 1)
            src = (row & (n - 1)) * dil + lax.shift_right_logical(row, _log2(n))
            perm = jnp.where(col == src, 1.0, 0.0).astype(BF16)
            for c in range(tm // MXU_DIM):
                hc = jnp.dot(perm, h[c * MXU_DIM:(c + 1) * MXU_DIM, :],
                             preferred_element_type=F32).astype(BF16)
                for r in range(dil):
                    h_ref[r * per + c * n:r * per + (c + 1) * n, :] = hc[r * n:(r + 1) * n, :]

    res = jnp.dot(h_ref[...], w_ref[...].astype(BF16),
                  preferred_element_type=F32).astype(o_ref.dtype)
    for r in range(dil):
        o_ref[r] = res[r * per:(r + 1) * per, :]


def _norm_matmul(x2d, gain, w_all, layer, *, batch, dil, col0, n_cols, tm, tn):
    T, D = x2d.shape
    S = T // batch
    assert S % tm == 0 and n_cols % tn == 0 and col0 % tn == 0
    assert tm % MXU_DIM == 0 and (MXU_DIM // dil) % BF16_ROWS == 0
    tiles = S // tm
    return pl.pallas_call(
        functools.partial(_norm_matmul_kernel, dil=dil),
        grid=(T // tm, n_cols // tn),
        in_specs=[
            pl.BlockSpec((tm, D), lambda m, j: (m, 0)),
            pl.BlockSpec((1, D), lambda m, j: (0, 0)),
            pl.BlockSpec((None, D, tn), lambda m, j: (layer, 0, col0 // tn + j)),
        ],
        out_specs=pl.BlockSpec((None, dil, tm // dil, tn),
                               lambda m, j: (m // tiles, 0, m % tiles, j)),
        out_shape=jax.ShapeDtypeStruct((batch, dil, S // dil, n_cols), BF16),
        scratch_shapes=[pltpu.VMEM((tm, D), BF16)],
        compiler_params=_params(("parallel", "arbitrary")),
        name="norm_matmul",
    )(x2d, gain.reshape(1, D), w_all)


def _halo_maps(tm, n_rows, rows):
    per = tm // rows
    last = n_rows // rows - 1
    prev_map = lambda i: jnp.maximum(i * per - 1, 0)
    next_map = lambda i: jnp.minimum((i + 1) * per, last)
    return prev_map, next_map


def _fill_halo_norm(x_ref, xp_ref, xn_ref, g_ref, h_ref, tiles_per_seq):
    i = pl.program_id(0)
    tm = x_ref.shape[0]
    halo = BF16_ROWS

    @pl.when(pl.program_id(1) == 0)
    def _():
        first = (i % tiles_per_seq) == 0
        last = ((i + 1) % tiles_per_seq) == 0
        gain = g_ref[...]
        h_ref[0:halo, :] = jnp.where(first, 0.0, _rmsnorm(xp_ref[...], gain)).astype(BF16)
        h_ref[halo:halo + tm, :] = _rmsnorm(x_ref[...], gain).astype(BF16)
        h_ref[halo + tm:, :] = jnp.where(last, 0.0, _rmsnorm(xn_ref[...], gain)).astype(BF16)


def _project(h_ref, w_ref):
    return jnp.dot(h_ref[...], w_ref[...].astype(BF16), preferred_element_type=F32)


def _dwconv3(r, cw_ref, cb_ref):
    rows = r.shape[0]
    return (pltpu.roll(r, 1, 0) * cw_ref[0:1, :] + r * cw_ref[1:2, :]
            + pltpu.roll(r, rows - 1, 0) * cw_ref[2:3, :] + cb_ref[...])


def _norm_in_gate_kernel(x_ref, xp_ref, xn_ref, g_ref, wu_ref, wb_ref, wc_ref, cw_ref, cb_ref,
                         o_ref, h_ref, *, tiles_per_seq):
    _fill_halo_norm(x_ref, xp_ref, xn_ref, g_ref, h_ref, tiles_per_seq)
    tm = x_ref.shape[0]
    v = _project(h_ref, wc_ref) * _project(h_ref, wu_ref)
    y = _project(h_ref, wb_ref) * _dwconv3(v, cw_ref, cb_ref)
    o_ref[...] = y[BF16_ROWS:BF16_ROWS + tm, :].astype(o_ref.dtype)


def _norm_up_gate_kernel(x_ref, xp_ref, xn_ref, g_ref, wa_ref, wb_ref, cwa_ref, cwb_ref,
                         cba_ref, cbb_ref, o_ref, h_ref, *, tiles_per_seq):
    _fill_halo_norm(x_ref, xp_ref, xn_ref, g_ref, h_ref, tiles_per_seq)
    tm = x_ref.shape[0]
    a = _dwconv3(_project(h_ref, wa_ref), cwa_ref, cba_ref)
    b = _dwconv3(_project(h_ref, wb_ref), cwb_ref, cbb_ref)
    gated = (a / (1.0 + jnp.exp(-a))) * b
    o_ref[...] = gated[BF16_ROWS:BF16_ROWS + tm, :].astype(o_ref.dtype)


def _norm_in_gate(x2d, gain, w_in_all, layer, conv_w, conv_b, *, seq_len, tm, tn):
    T, D = x2d.shape
    assert seq_len % tm == 0 and D % tn == 0
    nj = D // tn
    prev_map, next_map = _halo_maps(tm, T, BF16_ROWS)

    def weight(part):
        return pl.BlockSpec((None, D, tn), lambda i, j: (layer, 0, part * nj + j))

    return pl.pallas_call(
        functools.partial(_norm_in_gate_kernel, tiles_per_seq=seq_len // tm),
        grid=(T // tm, nj),
        in_specs=[
            pl.BlockSpec((tm, D), lambda i, j: (i, 0)),
            pl.BlockSpec((BF16_ROWS, D), lambda i, j: (prev_map(i), 0)),
            pl.BlockSpec((BF16_ROWS, D), lambda i, j: (next_map(i), 0)),
            pl.BlockSpec((1, D), lambda i, j: (0, 0)),
            weight(0), weight(1), weight(2),
            pl.BlockSpec((3, tn), lambda i, j: (0, j)),
            pl.BlockSpec((1, tn), lambda i, j: (0, j)),
        ],
        out_specs=pl.BlockSpec((tm, tn), lambda i, j: (i, j)),
        out_shape=jax.ShapeDtypeStruct((T, D), BF16),
        scratch_shapes=[pltpu.VMEM((tm + 2 * BF16_ROWS, D), BF16)],
        compiler_params=_params(("parallel", "arbitrary")),
        name="norm_in_gate",
    )(x2d, x2d, x2d, gain.reshape(1, D), w_in_all, w_in_all, w_in_all, conv_w,
      conv_b.reshape(1, D))


def _norm_up_gate(x2d, gain, w_up_all, layer, conv_w, conv_b, *, seq_len, tm, tn):
    T, D = x2d.shape
    F = w_up_all.shape[2] // 2
    assert seq_len % tm == 0 and F % tn == 0
    nj = F // tn
    prev_map, next_map = _halo_maps(tm, T, BF16_ROWS)
    cb = conv_b.reshape(1, 2 * F)

    def chan(rows, off):
        return pl.BlockSpec((rows, tn), lambda i, j: (0, off + j))

    def weight(off):
        return pl.BlockSpec((None, D, tn), lambda i, j: (layer, 0, off + j))

    return pl.pallas_call(
        functools.partial(_norm_up_gate_kernel, tiles_per_seq=seq_len // tm),
        grid=(T // tm, nj),
        in_specs=[
            pl.BlockSpec((tm, D), lambda i, j: (i, 0)),
            pl.BlockSpec((BF16_ROWS, D), lambda i, j: (prev_map(i), 0)),
            pl.BlockSpec((BF16_ROWS, D), lambda i, j: (next_map(i), 0)),
            pl.BlockSpec((1, D), lambda i, j: (0, 0)),
            weight(0), weight(nj), chan(3, 0), chan(3, nj), chan(1, 0), chan(1, nj),
        ],
        out_specs=pl.BlockSpec((tm, tn), lambda i, j: (i, j)),
        out_shape=jax.ShapeDtypeStruct((T, F), BF16),
        scratch_shapes=[pltpu.VMEM((tm + 2 * BF16_ROWS, D), BF16)],
        compiler_params=_params(("parallel", "arbitrary")),
        name="norm_up_gate",
    )(x2d, x2d, x2d, gain.reshape(1, D), w_up_all, w_up_all, conv_w, conv_w, cb, cb)


def _down_residual_kernel(g_ref, w_ref, x_ref, gain_ref, o_ref, *, final_norm):
    y = x_ref[...] + jnp.dot(g_ref[...], w_ref[...], preferred_element_type=F32)
    o_ref[...] = _rmsnorm(y, gain_ref[...]) if final_norm else y


def _down_residual(g, w_down, x2d, final_gain, *, tm):
    T, D = x2d.shape
    F = w_down.shape[0]
    assert T % tm == 0
    final_norm = final_gain is not None
    gain = final_gain if final_norm else jnp.ones((D,), F32)
    return pl.pallas_call(
        functools.partial(_down_residual_kernel, final_norm=final_norm),
        grid=(T // tm,),
        in_specs=[
            pl.BlockSpec((tm, F), lambda i: (i, 0)),
            pl.BlockSpec((F, D), lambda i: (0, 0), pipeline_mode=pl.Buffered(1)),
            pl.BlockSpec((tm, D), lambda i: (i, 0)),
            pl.BlockSpec((1, D), lambda i: (0, 0)),
        ],
        out_specs=pl.BlockSpec((tm, D), lambda i: (i, 0)),
        out_shape=jax.ShapeDtypeStruct((T, D), F32),
        compiler_params=_params(("parallel",)),
        name="down_residual",
    )(g, w_down, x2d, gain.reshape(1, D))


def _banded_attention_kernel(slope_ref, q_ref, k_ref, kp_ref, kn_ref, v_ref, vp_ref, vn_ref,
                             acc_ref, m_ref, lse_ref, kbuf, vbuf, bias_ref, slab_ref,
                             mstat_ref, lstat_ref, *, dil, seq, half, qs):
    lq = q_ref.shape[1]
    dh = q_ref.shape[2] // ATTN_HEADS
    n_sb = lq // qs
    it, hg = pl.program_id(1), pl.program_id(2)

    @pl.when(hg == 0)
    def _():
        mstat_ref[...] = jnp.zeros(mstat_ref.shape, F32)
        lstat_ref[...] = jnp.zeros(lstat_ref.shape, F32)

    for buf, prev, main, nxt in ((kbuf, kp_ref, k_ref, kn_ref), (vbuf, vp_ref, v_ref, vn_ref)):
        buf[:, 0:half, :] = prev[...]
        buf[:, half:half + lq, :] = main[...]
        buf[:, half + lq:, :] = nxt[...]

    win = qs + 2 * half
    ii = lax.broadcasted_iota(jnp.int32, (qs, win), 0)
    jj = lax.broadcasted_iota(jnp.int32, (qs, win), 1)
    delta = jnp.abs(jj - half - ii)
    dist = (dil * delta).astype(F32)
    lane = lax.broadcasted_iota(jnp.int32, (qs, LANES), 1)
    head0 = hg * ATTN_HEADS
    own_lanes = (lane >= head0) & (lane < head0 + ATTN_HEADS)
    scale2 = dh ** -0.5 * LOG2E
    neg_slopes2 = [-slope_ref[head0 + hh] * LOG2E for hh in range(ATTN_HEADS)]

    variants = sorted({(sb == 0, sb == n_sb - 1) for sb in range(n_sb)})
    for vi, (is_first, is_last) in enumerate(variants):
        masked = delta > half
        if is_first:
            masked = masked | ((it == 0) & (jj < half))
        if is_last:
            masked = masked | ((it == seq // lq - 1) & (jj >= qs + half))
        for hh in range(ATTN_HEADS):
            bias_ref[vi * ATTN_HEADS + hh] = jnp.where(masked, NEG_INF, neg_slopes2[hh] * dist)

    for sb in range(n_sb):
        vi = variants.index((sb == 0, sb == n_sb - 1))
        for r in range(dil):
            rows = pl.ds(sb * qs, qs) if dil == 1 else pl.ds(r + sb * qs * dil, qs, stride=dil)
            m_tile = jnp.zeros((qs, LANES), F32)
            den_tile = jnp.ones((qs, LANES), F32)
            for hh in range(ATTN_HEADS):
                cols = slice(hh * dh, (hh + 1) * dh)
                q = q_ref[r, sb * qs:(sb + 1) * qs, cols]
                kw = kbuf[r, sb * qs:sb * qs + win, cols]
                vw = vbuf[r, sb * qs:sb * qs + win, cols]
                t = lax.dot_general(q, kw, (((1,), (1,)), ((), ())),
                                    preferred_element_type=F32) * scale2
                t = t + bias_ref[vi * ATTN_HEADS + hh]
                m = jnp.max(t, axis=-1, keepdims=True)
                p = jnp.exp2(t - m)
                den = jnp.sum(p, axis=-1, keepdims=True)
                slab_ref[hh, rows, :] = jnp.dot(p.astype(BF16), vw, preferred_element_type=F32)
                m_tile = jnp.where(lane == head0 + hh, m, m_tile)
                den_tile = jnp.where(lane == head0 + hh, den, den_tile)
            srows = pl.ds(r * lq + sb * qs, qs)
            mstat_ref[srows, :] = jnp.where(own_lanes, m_tile, mstat_ref[srows, :])
            lstat_ref[srows, :] = jnp.where(own_lanes, m_tile + jnp.log2(den_tile),
                                            lstat_ref[srows, :])

    for hh in range(ATTN_HEADS):
        acc_ref[:, hh * dh:(hh + 1) * dh] = slab_ref[hh].astype(acc_ref.dtype)

    @pl.when(hg == pl.num_programs(2) - 1)
    def _():
        for r in range(dil):
            rows = pl.ds(0, lq) if dil == 1 else pl.ds(r, lq, stride=dil)
            m_ref[rows, :] = mstat_ref[r * lq:(r + 1) * lq, :]
            lse_ref[rows, :] = lstat_ref[r * lq:(r + 1) * lq, :]


def _banded_attention(qkv, slopes, *, window, dil):
    B, d, L, D3 = qkv.shape
    D = D3 // 3
    dh = D // N_HEADS
    half = (window // 2) // dil
    qs = LANES
    lq = max(ATTN_TOKENS // dil, qs)
    tokens = lq * dil
    wcols = ATTN_HEADS * dh
    n_hg = N_HEADS // ATTN_HEADS
    assert d == dil and L % lq == 0 and lq % half == 0 and half % BF16_ROWS == 0
    n_t = L // lq
    per = lq // half
    last = L // half - 1

    def main(c):
        return pl.BlockSpec((None, dil, lq, wcols), lambda b, i, g: (b, 0, i, c * n_hg + g))

    def prev(c):
        return pl.BlockSpec((None, dil, half, wcols),
                            lambda b, i, g: (b, 0, jnp.maximum(i * per - 1, 0), c * n_hg + g))

    def nxt(c):
        return pl.BlockSpec((None, dil, half, wcols),
                            lambda b, i, g: (b, 0, jnp.minimum((i + 1) * per, last), c * n_hg + g))

    kern = functools.partial(_banded_attention_kernel, dil=dil, seq=L, half=half, qs=qs)
    T = B * L * dil
    return pl.pallas_call(
        kern,
        grid=(B, n_t, n_hg),
        in_specs=[pl.BlockSpec(memory_space=pltpu.SMEM),
                  main(0), main(1), prev(1), nxt(1), main(2), prev(2), nxt(2)],
        out_specs=[pl.BlockSpec((tokens, wcols), lambda b, i, g: (b * n_t + i, g)),
                   pl.BlockSpec((tokens, LANES), lambda b, i, g: (b * n_t + i, 0)),
                   pl.BlockSpec((tokens, LANES), lambda b, i, g: (b * n_t + i, 0))],
        out_shape=[jax.ShapeDtypeStruct((T, D), BF16),
                   jax.ShapeDtypeStruct((T, LANES), F32),
                   jax.ShapeDtypeStruct((T, LANES), F32)],
        scratch_shapes=[pltpu.VMEM((dil, lq + 2 * half, wcols), BF16),
                        pltpu.VMEM((dil, lq + 2 * half, wcols), BF16),
                        pltpu.VMEM((min(lq // qs, 3) * ATTN_HEADS, qs, qs + 2 * half), F32),
                        pltpu.VMEM((ATTN_HEADS, tokens, dh), F32),
                        pltpu.VMEM((tokens, LANES), F32),
                        pltpu.VMEM((tokens, LANES), F32)],
        compiler_params=_params(("parallel", "parallel", "arbitrary")),
        name="banded_attention",
    )(slopes, qkv, qkv, qkv, qkv, qkv, qkv, qkv)


def _combine_out_kernel(*refs, n_groups):
    acc_refs = refs[:n_groups]
    m_refs = refs[n_groups:2 * n_groups]
    l_refs = refs[2 * n_groups:3 * n_groups]
    x_ref, w_ref, out_ref, mix_ref = refs[3 * n_groups:]
    dh = x_ref.shape[1] // N_HEADS
    lses = [r[...] for r in l_refs]
    top = functools.reduce(jnp.maximum, lses)
    total = functools.reduce(lambda a, b: a + b, [jnp.exp2(l - top) for l in lses])
    lse_all = top + jnp.log2(total)
    coefs = [jnp.exp2(r[...] - lse_all) for r in m_refs]
    for h in range(N_HEADS):
        cols = slice(h * dh, (h + 1) * dh)
        mixed = functools.reduce(
            lambda a, b: a + b,
            [c[:, h:h + 1] * acc[:, cols].astype(F32) for c, acc in zip(coefs, acc_refs)])
        mix_ref[:, cols] = mixed.astype(mix_ref.dtype)
    out_ref[...] = x_ref[...] + jnp.dot(mix_ref[...], w_ref[...],
                                        preferred_element_type=F32)


def _combine_out(accs, maxes, lses, x2d, w_out, *, tm):
    T, D = x2d.shape
    n = len(accs)
    return pl.pallas_call(
        functools.partial(_combine_out_kernel, n_groups=n),
        grid=(T // tm,),
        in_specs=([pl.BlockSpec((tm, D), lambda i: (i, 0))] * n
                  + [pl.BlockSpec((tm, LANES), lambda i: (i, 0))] * (2 * n)
                  + [pl.BlockSpec((tm, D), lambda i: (i, 0)),
                     pl.BlockSpec((D, D), lambda i: (0, 0))]),
        out_specs=pl.BlockSpec((tm, D), lambda i: (i, 0)),
        out_shape=jax.ShapeDtypeStruct((T, D), F32),
        scratch_shapes=[pltpu.VMEM((tm, D), BF16)],
        compiler_params=_params(("parallel",)),
        name="combine_out",
    )(*accs, *maxes, *lses, x2d, w_out)


def kernel(x, mix_norm_g, ffn_norm_g, final_norm_g, sc_w_in, sc_conv_w, sc_conv_b, sc_w_out,
           attn_w_qkv, attn_w_out, ffn_w_up, ffn_conv_w, ffn_conv_b, ffn_w_down):
    B, S, D = x.shape
    T = B * S
    depth = mix_norm_g.shape[0]
    n_mixers = 2
    slopes = jnp.asarray(2.0 ** (-ALIBI_MAX * np.arange(1, N_HEADS + 1) / N_HEADS), dtype=F32)
    xf = x.reshape(T, D)
    for i in range(depth):
        j = i // n_mixers
        if i % n_mixers == 0:
            y = _norm_in_gate(xf, mix_norm_g[i], sc_w_in, j, sc_conv_w[j], sc_conv_b[j],
                              seq_len=S, tm=1024, tn=512)
            xf = _down_residual(y, sc_w_out[j].astype(BF16), xf, None, tm=512)
        else:
            parts = []
            for g, (window, dil) in enumerate(DILATED_GROUPS):
                qkv = _norm_matmul(xf, mix_norm_g[i], attn_w_qkv, j, batch=B, dil=dil,
                                   col0=g * 3 * D, n_cols=3 * D, tm=1024, tn=1024)
                parts.append(_banded_attention(qkv, slopes, window=window, dil=dil))
            accs, maxes, lses = zip(*parts)
            xf = _combine_out(accs, maxes, lses, xf, attn_w_out[j].astype(BF16), tm=256)
        g = _norm_up_gate(xf, ffn_norm_g[i], ffn_w_up, i, ffn_conv_w[i],
                          ffn_conv_b[i], seq_len=S, tm=1024, tn=512)
        xf = _down_residual(g, ffn_w_down[i].astype(BF16), xf,
                            final_norm_g if i == depth - 1 else None, tm=512)
    return xf.reshape(B, S, D)
```

```python
import functools

import numpy as np
import jax
import jax.numpy as jnp
from jax import lax
from jax.experimental import pallas as pl
from jax.experimental.pallas import tpu as pltpu

N_HEADS = 16
DILATED_GROUPS = ((128, 1), (512, 4), (2048, 16))
NORM_EPS = 1e-5
ALIBI_MAX = 8.0
NEG_INF = -1e30
LOG2E = 1.4426950408889634

LANES = 128
BF16_ROWS = 16
MXU_DIM = 256
VMEM_LIMIT = 56 * 1024 * 1024
SIDE_ROWS = BF16_ROWS
ATTN_TOKENS = 1024
ATTN_HEADS = 4

F32 = jnp.float32
BF16 = jnp.bfloat16


def _params(semantics):
    return pltpu.CompilerParams(dimension_semantics=semantics,
                                vmem_limit_bytes=VMEM_LIMIT)


def _rmsnorm(x, gain):
    ms = jnp.mean(x * x, axis=-1, keepdims=True)
    return x * lax.rsqrt(ms + NORM_EPS) * gain


def _log2(n):
    assert n & (n - 1) == 0
    return n.bit_length() - 1


def _norm_matmul_kernel(x_ref, g_ref, w_ref, o_ref, h_ref, *, dil):
    tm = x_ref.shape[0]
    per = tm // dil

    @pl.when(pl.program_id(1) == 0)
    def _():
        h = _rmsnorm(x_ref[...], g_ref[...]).astype(BF16)
        if dil == 1:
            h_ref[...] = h
        else:
            n = MXU_DIM // dil
            row = lax.broadcasted_iota(jnp.int32, (MXU_DIM, MXU_DIM), 0)
            col = lax.broadcasted_iota(jnp.int32, (MXU_DIM, MXU_DIM), 1)
            src = (row & (n - 1)) * dil + lax.shift_right_logical(row, _log2(n))
            perm = jnp.where(col == src, 1.0, 0.0).astype(BF16)
            for c in range(tm // MXU_DIM):
                hc = jnp.dot(perm, h[c * MXU_DIM:(c + 1) * MXU_DIM, :],
                             preferred_element_type=F32).astype(BF16)
                for r in range(dil):
                    h_ref[r * per + c * n:r * per + (c + 1) * n, :] = hc[r * n:(r + 1) * n, :]

    res = jnp.dot(h_ref[...], w_ref[...].astype(BF16),
                  preferred_element_type=F32).astype(o_ref.dtype)
    for r in range(dil):
        o_ref[r] = res[r * per:(r + 1) * per, :]


def _norm_matmul(x2d, gain, w_all, layer, *, batch, dil, col0, n_cols, tm, tn):
    T, D = x2d.shape
    S = T // batch
    assert S % tm == 0 and n_cols % tn == 0 and col0 % tn == 0
    assert tm % MXU_DIM == 0 and (MXU_DIM // dil) % BF16_ROWS == 0
    tiles = S // tm
    return pl.pallas_call(
        functools.partial(_norm_matmul_kernel, dil=dil),
        grid=(T // tm, n_cols // tn),
        in_specs=[
            pl.BlockSpec((tm, D), lambda m, j: (m, 0)),
            pl.BlockSpec((1, D), lambda m, j: (0, 0)),
            pl.BlockSpec((None, D, tn), lambda m, j: (layer, 0, col0 // tn + j)),
        ],
        out_specs=pl.BlockSpec((None, dil, tm // dil, tn),
                               lambda m, j: (m // tiles, 0, m % tiles, j)),
        out_shape=jax.ShapeDtypeStruct((batch, dil, S // dil, n_cols), BF16),
        scratch_shapes=[pltpu.VMEM((tm, D), BF16)],
        compiler_params=_params(("parallel", "arbitrary")),
        name="norm_matmul",
    )(x2d, gain.reshape(1, D), w_all)


def _halo_maps(tm, n_rows, rows):
    per = tm // rows
    last = n_rows // rows - 1
    prev_map = lambda i: jnp.maximum(i * per - 1, 0)
    next_map = lambda i: jnp.minimum((i + 1) * per, last)
    return prev_map, next_map


def _fill_halo_norm(x_ref, xp_ref, xn_ref, g_ref, h_ref, tiles_per_seq):
    i = pl.program_id(0)
    tm = x_ref.shape[0]
    halo = BF16_ROWS

    @pl.when(pl.program_id(1) == 0)
    def _():
        first = (i % tiles_per_seq) == 0
        last = ((i + 1) % tiles_per_seq) == 0
        gain = g_ref[...]
        h_ref[0:halo, :] = jnp.where(first, 0.0, _rmsnorm(xp_ref[...], gain)).astype(BF16)
        h_ref[halo:halo + tm, :] = _rmsnorm(x_ref[...], gain).astype(BF16)
        h_ref[halo + tm:, :] = jnp.where(last, 0.0, _rmsnorm(xn_ref[...], gain)).astype(BF16)


def _project(h_ref, w_ref):
    return jnp.dot(h_ref[...], w_ref[...].astype(BF16), preferred_element_type=F32)


def _dwconv3(r, cw_ref, cb_ref):
    rows = r.shape[0]
    return (pltpu.roll(r, 1, 0) * cw_ref[0:1, :] + r * cw_ref[1:2, :]
            + pltpu.roll(r, rows - 1, 0) * cw_ref[2:3, :] + cb_ref[...])


def _norm_in_gate_kernel(x_ref, xp_ref, xn_ref, g_ref, wu_ref, wb_ref, wc_ref, cw_ref, cb_ref,
                         o_ref, h_ref, *, tiles_per_seq):
    _fill_halo_norm(x_ref, xp_ref, xn_ref, g_ref, h_ref, tiles_per_seq)
    tm = x_ref.shape[0]
    v = _project(h_ref, wc_ref) * _project(h_ref, wu_ref)
    y = _project(h_ref, wb_ref) * _dwconv3(v, cw_ref, cb_ref)
    o_ref[...] = y[BF16_ROWS:BF16_ROWS + tm, :].astype(o_ref.dtype)


def _norm_up_gate_kernel(*refs, tiles_per_seq, n_side):
    (x_ref, xp_ref, xn_ref, g_ref, wa_ref, wb_ref,
     cwa_ref, cwb_ref, cba_ref, cbb_ref) = refs[:10]
    side_src = refs[10:10 + n_side]
    o_ref = refs[10 + n_side]
    side_dst = refs[11 + n_side:11 + 2 * n_side]
    h_ref = refs[-1]
    _fill_halo_norm(x_ref, xp_ref, xn_ref, g_ref, h_ref, tiles_per_seq)
    for src, dst in zip(side_src, side_dst):
        dst[...] = src[...].astype(dst.dtype)
    tm = x_ref.shape[0]
    a = _dwconv3(_project(h_ref, wa_ref), cwa_ref, cba_ref)
    b = _dwconv3(_project(h_ref, wb_ref), cwb_ref, cbb_ref)
    gated = (a / (1.0 + jnp.exp(-a))) * b
    o_ref[...] = gated[BF16_ROWS:BF16_ROWS + tm, :].astype(o_ref.dtype)


def _norm_in_gate(x2d, gain, w_in_all, layer, conv_w, conv_b, *, seq_len, tm, tn):
    T, D = x2d.shape
    assert seq_len % tm == 0 and D % tn == 0
    nj = D // tn
    prev_map, next_map = _halo_maps(tm, T, BF16_ROWS)

    def weight(part):
        return pl.BlockSpec((None, D, tn), lambda i, j: (layer, 0, part * nj + j))

    return pl.pallas_call(
        functools.partial(_norm_in_gate_kernel, tiles_per_seq=seq_len // tm),
        grid=(T // tm, nj),
        in_specs=[
            pl.BlockSpec((tm, D), lambda i, j: (i, 0)),
            pl.BlockSpec((BF16_ROWS, D), lambda i, j: (prev_map(i), 0)),
            pl.BlockSpec((BF16_ROWS, D), lambda i, j: (next_map(i), 0)),
            pl.BlockSpec((1, D), lambda i, j: (0, 0)),
            weight(0), weight(1), weight(2),
            pl.BlockSpec((3, tn), lambda i, j: (0, j)),
            pl.BlockSpec((1, tn), lambda i, j: (0, j)),
        ],
        out_specs=pl.BlockSpec((tm, tn), lambda i, j: (i, j)),
        out_shape=jax.ShapeDtypeStruct((T, D), BF16),
        scratch_shapes=[pltpu.VMEM((tm + 2 * BF16_ROWS, D), BF16)],
        compiler_params=_params(("parallel", "arbitrary")),
        name="norm_in_gate",
    )(x2d, x2d, x2d, gain.reshape(1, D), w_in_all, w_in_all, w_in_all, conv_w,
      conv_b.reshape(1, D))


def _norm_up_gate(x2d, gain, w_up_all, layer, conv_w, conv_b, side_casts, *, seq_len, tm, tn):
    T, D = x2d.shape
    F = w_up_all.shape[2] // 2
    assert seq_len % tm == 0 and F % tn == 0
    nj = F // tn
    steps = (T // tm) * nj
    prev_map, next_map = _halo_maps(tm, T, BF16_ROWS)
    cb = conv_b.reshape(1, 2 * F)

    def chan(rows, off):
        return pl.BlockSpec((rows, tn), lambda i, j: (0, off + j))

    def weight(off):
        return pl.BlockSpec((None, D, tn), lambda i, j: (layer, 0, off + j))

    side_in, side_out, side_shapes = [], [], []
    for w_all, idx in side_casts:
        _, rows, cols = w_all.shape
        blocks = rows // SIDE_ROWS
        assert rows % SIDE_ROWS == 0 and blocks <= steps
        blk = lambda i, j, blocks=blocks: jnp.minimum(i * nj + j, blocks - 1)
        side_in.append(pl.BlockSpec((None, SIDE_ROWS, cols),
                                    lambda i, j, idx=idx, blk=blk: (idx, blk(i, j), 0)))
        side_out.append(pl.BlockSpec((None, SIDE_ROWS, cols),
                                     lambda i, j, blk=blk: (0, blk(i, j), 0)))
        side_shapes.append(jax.ShapeDtypeStruct((1, rows, cols), BF16))

    outs = pl.pallas_call(
        functools.partial(_norm_up_gate_kernel, tiles_per_seq=seq_len // tm,
                          n_side=len(side_casts)),
        grid=(T // tm, nj),
        in_specs=[
            pl.BlockSpec((tm, D), lambda i, j: (i, 0)),
            pl.BlockSpec((BF16_ROWS, D), lambda i, j: (prev_map(i), 0)),
            pl.BlockSpec((BF16_ROWS, D), lambda i, j: (next_map(i), 0)),
            pl.BlockSpec((1, D), lambda i, j: (0, 0)),
            weight(0), weight(nj), chan(3, 0), chan(3, nj), chan(1, 0), chan(1, nj),
        ] + side_in,
        out_specs=[pl.BlockSpec((tm, tn), lambda i, j: (i, j))] + side_out,
        out_shape=[jax.ShapeDtypeStruct((T, F), BF16)] + side_shapes,
        scratch_shapes=[pltpu.VMEM((tm + 2 * BF16_ROWS, D), BF16)],
        compiler_params=_params(("arbitrary", "arbitrary")),
        name="norm_up_gate",
    )(x2d, x2d, x2d, gain.reshape(1, D), w_up_all, w_up_all, conv_w, conv_w, cb, cb,
      *[w for w, _ in side_casts])
    return outs[0], list(outs[1:])


def _down_residual_kernel(g_ref, w_ref, x_ref, gain_ref, o_ref, *, final_norm):
    y = x_ref[...] + jnp.dot(g_ref[...], w_ref[...], preferred_element_type=F32)
    o_ref[...] = _rmsnorm(y, gain_ref[...]) if final_norm else y


def _down_residual(g, w_down, x2d, final_gain, *, tm):
    T, D = x2d.shape
    F = w_down.shape[0]
    assert T % tm == 0
    final_norm = final_gain is not None
    gain = final_gain if final_norm else jnp.ones((D,), F32)
    return pl.pallas_call(
        functools.partial(_down_residual_kernel, final_norm=final_norm),
        grid=(T // tm,),
        in_specs=[
            pl.BlockSpec((tm, F), lambda i: (i, 0)),
            pl.BlockSpec((F, D), lambda i: (0, 0), pipeline_mode=pl.Buffered(1)),
            pl.BlockSpec((tm, D), lambda i: (i, 0)),
            pl.BlockSpec((1, D), lambda i: (0, 0)),
        ],
        out_specs=pl.BlockSpec((tm, D), lambda i: (i, 0)),
        out_shape=jax.ShapeDtypeStruct((T, D), F32),
        compiler_params=_params(("parallel",)),
        name="down_residual",
    )(g, w_down, x2d, gain.reshape(1, D))


def _banded_attention_kernel(slope_ref, q_ref, k_ref, kp_ref, kn_ref, v_ref, vp_ref, vn_ref,
                             acc_ref, m_ref, lse_ref, kbuf, vbuf, bias_ref, slab_ref,
                             mstat_ref, lstat_ref, *, dil, seq, half, qs):
    lq = q_ref.shape[1]
    dh = q_ref.shape[2] // ATTN_HEADS
    n_sb = lq // qs
    it, hg = pl.program_id(1), pl.program_id(2)

    @pl.when(hg == 0)
    def _():
        mstat_ref[...] = jnp.zeros(mstat_ref.shape, F32)
        lstat_ref[...] = jnp.zeros(lstat_ref.shape, F32)

    for buf, prev, main, nxt in ((kbuf, kp_ref, k_ref, kn_ref), (vbuf, vp_ref, v_ref, vn_ref)):
        buf[:, 0:half, :] = prev[...]
        buf[:, half:half + lq, :] = main[...]
        buf[:, half + lq:, :] = nxt[...]

    win = qs + 2 * half
    ii = lax.broadcasted_iota(jnp.int32, (qs, win), 0)
    jj = lax.broadcasted_iota(jnp.int32, (qs, win), 1)
    delta = jnp.abs(jj - half - ii)
    dist = (dil * delta).astype(F32)
    lane = lax.broadcasted_iota(jnp.int32, (qs, LANES), 1)
    head0 = hg * ATTN_HEADS
    own_lanes = (lane >= head0) & (lane < head0 + ATTN_HEADS)
    scale2 = dh ** -0.5 * LOG2E
    neg_slopes2 = [-slope_ref[head0 + hh] * LOG2E for hh in range(ATTN_HEADS)]

    variants = sorted({(sb == 0, sb == n_sb - 1) for sb in range(n_sb)})
    for vi, (is_first, is_last) in enumerate(variants):
        masked = delta > half
        if is_first:
            masked = masked | ((it == 0) & (jj < half))
        if is_last:
            masked = masked | ((it == seq // lq - 1) & (jj >= qs + half))
        for hh in range(ATTN_HEADS):
            bias_ref[vi * ATTN_HEADS + hh] = jnp.where(masked, NEG_INF, neg_slopes2[hh] * dist)

    for sb in range(n_sb):
        vi = variants.index((sb == 0, sb == n_sb - 1))
        for r in range(dil):
            rows = pl.ds(sb * qs, qs) if dil == 1 else pl.ds(r + sb * qs * dil, qs, stride=dil)
            m_tile = jnp.zeros((qs, LANES), F32)
            den_tile = jnp.ones((qs, LANES), F32)
            for hh in range(ATTN_HEADS):
                cols = slice(hh * dh, (hh + 1) * dh)
                q = q_ref[r, sb * qs:(sb + 1) * qs, cols]
                kw = kbuf[r, sb * qs:sb * qs + win, cols]
                vw = vbuf[r, sb * qs:sb * qs + win, cols]
                t = lax.dot_general(q, kw, (((1,), (1,)), ((), ())),
                                    preferred_element_type=F32) * scale2
                t = t + bias_ref[vi * ATTN_HEADS + hh]
                m = jnp.max(t, axis=-1, keepdims=True)
                p = jnp.exp2(t - m)
                den = jnp.sum(p, axis=-1, keepdims=True)
                slab_ref[hh, rows, :] = jnp.dot(p.astype(BF16), vw, preferred_element_type=F32)
                m_tile = jnp.where(lane == head0 + hh, m, m_tile)
                den_tile = jnp.where(lane == head0 + hh, den, den_tile)
            srows = pl.ds(r * lq + sb * qs, qs)
            mstat_ref[srows, :] = jnp.where(own_lanes, m_tile, mstat_ref[srows, :])
            lstat_ref[srows, :] = jnp.where(own_lanes, m_tile + jnp.log2(den_tile),
                                            lstat_ref[srows, :])

    for hh in range(ATTN_HEADS):
        acc_ref[:, hh * dh:(hh + 1) * dh] = slab_ref[hh].astype(acc_ref.dtype)

    @pl.when(hg == pl.num_programs(2) - 1)
    def _():
        for r in range(dil):
            rows = pl.ds(0, lq) if dil == 1 else pl.ds(r, lq, stride=dil)
            m_ref[rows, :] = mstat_ref[r * lq:(r + 1) * lq, :]
            lse_ref[rows, :] = lstat_ref[r * lq:(r + 1) * lq, :]


def _banded_attention(qkv, slopes, *, window, dil):
    B, d, L, D3 = qkv.shape
    D = D3 // 3
    dh = D // N_HEADS
    half = (window // 2) // dil
    qs = LANES
    lq = max(ATTN_TOKENS // dil, qs)
    tokens = lq * dil
    wcols = ATTN_HEADS * dh
    n_hg = N_HEADS // ATTN_HEADS
    assert d == dil and L % lq == 0 and lq % half == 0 and half % BF16_ROWS == 0
    n_t = L // lq
    per = lq // half
    last = L // half - 1

    def main(c):
        return pl.BlockSpec((None, dil, lq, wcols), lambda b, i, g: (b, 0, i, c * n_hg + g))

    def prev(c):
        return pl.BlockSpec((None, dil, half, wcols),
                            lambda b, i, g: (b, 0, jnp.maximum(i * per - 1, 0), c * n_hg + g))

    def nxt(c):
        return pl.BlockSpec((None, dil, half, wcols),
                            lambda b, i, g: (b, 0, jnp.minimum((i + 1) * per, last), c * n_hg + g))

    kern = functools.partial(_banded_attention_kernel, dil=dil, seq=L, half=half, qs=qs)
    T = B * L * dil
    return pl.pallas_call(
        kern,
        grid=(B, n_t, n_hg),
        in_specs=[pl.BlockSpec(memory_space=pltpu.SMEM),
                  main(0), main(1), prev(1), nxt(1), main(2), prev(2), nxt(2)],
        out_specs=[pl.BlockSpec((tokens, wcols), lambda b, i, g: (b * n_t + i, g)),
                   pl.BlockSpec((tokens, LANES), lambda b, i, g: (b * n_t + i, 0)),
                   pl.BlockSpec((tokens, LANES), lambda b, i, g: (b * n_t + i, 0))],
        out_shape=[jax.ShapeDtypeStruct((T, D), BF16),
                   jax.ShapeDtypeStruct((T, LANES), F32),
                   jax.ShapeDtypeStruct((T, LANES), F32)],
        scratch_shapes=[pltpu.VMEM((dil, lq + 2 * half, wcols), BF16),
                        pltpu.VMEM((dil, lq + 2 * half, wcols), BF16),
                        pltpu.VMEM((min(lq // qs, 3) * ATTN_HEADS, qs, qs + 2 * half), F32),
                        pltpu.VMEM((ATTN_HEADS, tokens, dh), F32),
                        pltpu.VMEM((tokens, LANES), F32),
                        pltpu.VMEM((tokens, LANES), F32)],
        compiler_params=_params(("parallel", "parallel", "arbitrary")),
        name="banded_attention",
    )(slopes, qkv, qkv, qkv, qkv, qkv, qkv, qkv)


def _combine_out_kernel(*refs, n_groups):
    acc_refs = refs[:n_groups]
    m_refs = refs[n_groups:2 * n_groups]
    l_refs = refs[2 * n_groups:3 * n_groups]
    x_ref, w_ref, out_ref, mix_ref = refs[3 * n_groups:]
    dh = x_ref.shape[1] // N_HEADS
    lses = [r[...] for r in l_refs]
    top = functools.reduce(jnp.maximum, lses)
    total = functools.reduce(lambda a, b: a + b, [jnp.exp2(l - top) for l in lses])
    lse_all = top + jnp.log2(total)
    coefs = [jnp.exp2(r[...] - lse_all) for r in m_refs]
    for h in range(N_HEADS):
        cols = slice(h * dh, (h + 1) * dh)
        mixed = functools.reduce(
            lambda a, b: a + b,
            [c[:, h:h + 1] * acc[:, cols].astype(F32) for c, acc in zip(coefs, acc_refs)])
        mix_ref[:, cols] = mixed.astype(mix_ref.dtype)
    out_ref[...] = x_ref[...] + jnp.dot(mix_ref[...], w_ref[...],
                                        preferred_element_type=F32)


def _combine_out(accs, maxes, lses, x2d, w_out, *, tm):
    T, D = x2d.shape
    n = len(accs)
    return pl.pallas_call(
        functools.partial(_combine_out_kernel, n_groups=n),
        grid=(T // tm,),
        in_specs=([pl.BlockSpec((tm, D), lambda i: (i, 0))] * n
                  + [pl.BlockSpec((tm, LANES), lambda i: (i, 0))] * (2 * n)
                  + [pl.BlockSpec((tm, D), lambda i: (i, 0)),
                     pl.BlockSpec((D, D), lambda i: (0, 0))]),
        out_specs=pl.BlockSpec((tm, D), lambda i: (i, 0)),
        out_shape=jax.ShapeDtypeStruct((T, D), F32),
        scratch_shapes=[pltpu.VMEM((tm, D), BF16)],
        compiler_params=_params(("parallel",)),
        name="combine_out",
    )(*accs, *maxes, *lses, x2d, w_out)


def kernel(x, mix_norm_g, ffn_norm_g, final_norm_g, sc_w_in, sc_conv_w, sc_conv_b, sc_w_out,
           attn_w_qkv, attn_w_out, ffn_w_up, ffn_conv_w, ffn_conv_b, ffn_w_down):
    B, S, D = x.shape
    T = B * S
    depth = mix_norm_g.shape[0]
    n_mixers = 2
    slopes = jnp.asarray(2.0 ** (-ALIBI_MAX * np.arange(1, N_HEADS + 1) / N_HEADS), dtype=F32)
    xf = x.reshape(T, D)

    def mixer_weight(layer):
        stacked = sc_w_in if layer % n_mixers == 0 else attn_w_qkv
        return stacked, layer // n_mixers

    precast = {}
    for i in range(depth):
        j = i // n_mixers
        w_mix, k_mix = precast.get(("mix", i), mixer_weight(i))
        if i % n_mixers == 0:
            y = _norm_in_gate(xf, mix_norm_g[i], w_mix, k_mix, sc_conv_w[j], sc_conv_b[j],
                              seq_len=S, tm=1024, tn=512)
            xf = _down_residual(y, sc_w_out[j].astype(BF16), xf, None, tm=512)
        else:
            parts = []
            for g, (window, dil) in enumerate(DILATED_GROUPS):
                qkv = _norm_matmul(xf, mix_norm_g[i], w_mix, k_mix, batch=B, dil=dil,
                                   col0=g * 3 * D, n_cols=3 * D, tm=1024,
                                   tn=2048 if w_mix.dtype == BF16 else 1024)
                parts.append(_banded_attention(qkv, slopes, window=window, dil=dil))
            accs, maxes, lses = zip(*parts)
            xf = _combine_out(accs, maxes, lses, xf, attn_w_out[j].astype(BF16), tm=256)
        w_up, k_up = precast.get(("up", i), (ffn_w_up, i))
        side = [mixer_weight(i + 1), (ffn_w_up, i + 1)] if i + 1 < depth else []
        g, copies = _norm_up_gate(xf, ffn_norm_g[i], w_up, k_up, ffn_conv_w[i], ffn_conv_b[i],
                                  side, seq_len=S, tm=1024, tn=512)
        if side:
            precast[("mix", i + 1)] = (copies[0], 0)
            precast[("up", i + 1)] = (copies[1], 0)
        xf = _down_residual(g, ffn_w_down[i].astype(BF16), xf,
                            final_norm_g if i == depth - 1 else None, tm=512)
    return xf.reshape(B, S, D)
```

```python
import functools

import numpy as np
import jax
import jax.numpy as jnp
from jax import lax
from jax.experimental import pallas as pl
from jax.experimental.pallas import tpu as pltpu

N_HEADS = 16
DILATED_GROUPS = ((128, 1), (512, 4), (2048, 16))
NORM_EPS = 1e-5
ALIBI_MAX = 8.0
NEG_INF = -1e30
LOG2E = 1.4426950408889634

LANES = 128
BF16_ROWS = 16
MXU_DIM = 256
VMEM_LIMIT = 56 * 1024 * 1024
SIDE_ROWS = BF16_ROWS
ATTN_TOKENS = 1024
ATTN_HEADS = 4

F32 = jnp.float32
BF16 = jnp.bfloat16


def _params(semantics):
    return pltpu.CompilerParams(dimension_semantics=semantics,
                                vmem_limit_bytes=VMEM_LIMIT)


def _rmsnorm(x, gain):
    ms = jnp.mean(x * x, axis=-1, keepdims=True)
    return x * lax.rsqrt(ms + NORM_EPS) * gain


def _log2(n):
    assert n & (n - 1) == 0
    return n.bit_length() - 1


def _norm_matmul_kernel(x_ref, g_ref, w_ref, o_ref, h_ref, *, dil):
    tm = x_ref.shape[0]
    per = tm // dil

    @pl.when(pl.program_id(1) == 0)
    def _():
        if x_ref.dtype == BF16:
            h = x_ref[...]
        else:
            h = _rmsnorm(x_ref[...], g_ref[...]).astype(BF16)
        if dil == 1:
            h_ref[...] = h
        else:
            n = MXU_DIM // dil
            row = lax.broadcasted_iota(jnp.int32, (MXU_DIM, MXU_DIM), 0)
            col = lax.broadcasted_iota(jnp.int32, (MXU_DIM, MXU_DIM), 1)
            src = (row & (n - 1)) * dil + lax.shift_right_logical(row, _log2(n))
            perm = jnp.where(col == src, 1.0, 0.0).astype(BF16)
            for c in range(tm // MXU_DIM):
                hc = jnp.dot(perm, h[c * MXU_DIM:(c + 1) * MXU_DIM, :],
                             preferred_element_type=F32).astype(BF16)
                for r in range(dil):
                    h_ref[r * per + c * n:r * per + (c + 1) * n, :] = hc[r * n:(r + 1) * n, :]

    res = jnp.dot(h_ref[...], w_ref[...].astype(BF16),
                  preferred_element_type=F32).astype(o_ref.dtype)
    for r in range(dil):
        o_ref[r] = res[r * per:(r + 1) * per, :]


def _norm_matmul(x2d, gain, w_all, layer, *, batch, dil, col0, n_cols, tm, tn):
    T, D = x2d.shape
    S = T // batch
    assert S % tm == 0 and n_cols % tn == 0 and col0 % tn == 0
    assert tm % MXU_DIM == 0 and (MXU_DIM // dil) % BF16_ROWS == 0
    tiles = S // tm
    return pl.pallas_call(
        functools.partial(_norm_matmul_kernel, dil=dil),
        grid=(T // tm, n_cols // tn),
        in_specs=[
            pl.BlockSpec((tm, D), lambda m, j: (m, 0)),
            pl.BlockSpec((1, D), lambda m, j: (0, 0)),
            pl.BlockSpec((None, D, tn), lambda m, j: (layer, 0, col0 // tn + j)),
        ],
        out_specs=pl.BlockSpec((None, dil, tm // dil, tn),
                               lambda m, j: (m // tiles, 0, m % tiles, j)),
        out_shape=jax.ShapeDtypeStruct((batch, dil, S // dil, n_cols), BF16),
        scratch_shapes=[pltpu.VMEM((tm, D), BF16)],
        compiler_params=_params(("parallel", "arbitrary")),
        name="norm_matmul",
    )(x2d, gain.reshape(1, D), w_all)


def _halo_maps(tm, n_rows, rows):
    per = tm // rows
    last = n_rows // rows - 1
    prev_map = lambda i: jnp.maximum(i * per - 1, 0)
    next_map = lambda i: jnp.minimum((i + 1) * per, last)
    return prev_map, next_map


def _fill_halo_norm(x_ref, xp_ref, xn_ref, g_ref, h_ref, tiles_per_seq):
    i = pl.program_id(0)
    tm = x_ref.shape[0]
    halo = BF16_ROWS

    @pl.when(pl.program_id(1) == 0)
    def _():
        first = (i % tiles_per_seq) == 0
        last = ((i + 1) % tiles_per_seq) == 0
        gain = g_ref[...]
        h_ref[0:halo, :] = jnp.where(first, 0.0, _rmsnorm(xp_ref[...], gain)).astype(BF16)
        h_ref[halo:halo + tm, :] = _rmsnorm(x_ref[...], gain).astype(BF16)
        h_ref[halo + tm:, :] = jnp.where(last, 0.0, _rmsnorm(xn_ref[...], gain)).astype(BF16)


def _project(h_ref, w_ref):
    return jnp.dot(h_ref[...], w_ref[...].astype(BF16), preferred_element_type=F32)


def _dwconv3(r, cw_ref, cb_ref):
    rows = r.shape[0]
    return (pltpu.roll(r, 1, 0) * cw_ref[0:1, :] + r * cw_ref[1:2, :]
            + pltpu.roll(r, rows - 1, 0) * cw_ref[2:3, :] + cb_ref[...])


def _norm_in_gate_kernel(x_ref, xp_ref, xn_ref, g_ref, wu_ref, wb_ref, wc_ref, cw_ref, cb_ref,
                         o_ref, h_ref, *, tiles_per_seq):
    _fill_halo_norm(x_ref, xp_ref, xn_ref, g_ref, h_ref, tiles_per_seq)
    tm = x_ref.shape[0]
    v = _project(h_ref, wc_ref) * _project(h_ref, wu_ref)
    y = _project(h_ref, wb_ref) * _dwconv3(v, cw_ref, cb_ref)
    o_ref[...] = y[BF16_ROWS:BF16_ROWS + tm, :].astype(o_ref.dtype)


def _norm_up_gate_kernel(*refs, tiles_per_seq, n_side):
    (x_ref, xp_ref, xn_ref, g_ref, wa_ref, wb_ref,
     cwa_ref, cwb_ref, cba_ref, cbb_ref) = refs[:10]
    side_src = refs[10:10 + n_side]
    o_ref = refs[10 + n_side]
    side_dst = refs[11 + n_side:11 + 2 * n_side]
    h_ref = refs[-1]
    _fill_halo_norm(x_ref, xp_ref, xn_ref, g_ref, h_ref, tiles_per_seq)
    for src, dst in zip(side_src, side_dst):
        dst[...] = src[...].astype(dst.dtype)
    tm = x_ref.shape[0]
    a = _dwconv3(_project(h_ref, wa_ref), cwa_ref, cba_ref)
    b = _dwconv3(_project(h_ref, wb_ref), cwb_ref, cbb_ref)
    gated = (a / (1.0 + jnp.exp(-a))) * b
    o_ref[...] = gated[BF16_ROWS:BF16_ROWS + tm, :].astype(o_ref.dtype)


def _norm_in_gate(x2d, gain, w_in_all, layer, conv_w, conv_b, *, seq_len, tm, tn):
    T, D = x2d.shape
    assert seq_len % tm == 0 and D % tn == 0
    nj = D // tn
    prev_map, next_map = _halo_maps(tm, T, BF16_ROWS)

    def weight(part):
        return pl.BlockSpec((None, D, tn), lambda i, j: (layer, 0, part * nj + j))

    return pl.pallas_call(
        functools.partial(_norm_in_gate_kernel, tiles_per_seq=seq_len // tm),
        grid=(T // tm, nj),
        in_specs=[
            pl.BlockSpec((tm, D), lambda i, j: (i, 0)),
            pl.BlockSpec((BF16_ROWS, D), lambda i, j: (prev_map(i), 0)),
            pl.BlockSpec((BF16_ROWS, D), lambda i, j: (next_map(i), 0)),
            pl.BlockSpec((1, D), lambda i, j: (0, 0)),
            weight(0), weight(1), weight(2),
            pl.BlockSpec((3, tn), lambda i, j: (0, j)),
            pl.BlockSpec((1, tn), lambda i, j: (0, j)),
        ],
        out_specs=pl.BlockSpec((tm, tn), lambda i, j: (i, j)),
        out_shape=jax.ShapeDtypeStruct((T, D), BF16),
        scratch_shapes=[pltpu.VMEM((tm + 2 * BF16_ROWS, D), BF16)],
        compiler_params=_params(("parallel", "arbitrary")),
        name="norm_in_gate",
    )(x2d, x2d, x2d, gain.reshape(1, D), w_in_all, w_in_all, w_in_all, conv_w,
      conv_b.reshape(1, D))


def _norm_up_gate(x2d, gain, w_up_all, layer, conv_w, conv_b, side_casts, *, seq_len, tm, tn):
    T, D = x2d.shape
    F = w_up_all.shape[2] // 2
    assert seq_len % tm == 0 and F % tn == 0
    nj = F // tn
    steps = (T // tm) * nj
    prev_map, next_map = _halo_maps(tm, T, BF16_ROWS)
    cb = conv_b.reshape(1, 2 * F)

    def chan(rows, off):
        return pl.BlockSpec((rows, tn), lambda i, j: (0, off + j))

    def weight(off):
        return pl.BlockSpec((None, D, tn), lambda i, j: (layer, 0, off + j))

    side_in, side_out, side_shapes = [], [], []
    for w_all, idx in side_casts:
        _, rows, cols = w_all.shape
        blocks = rows // SIDE_ROWS
        assert rows % SIDE_ROWS == 0 and blocks <= steps
        blk = lambda i, j, blocks=blocks: jnp.minimum(i * nj + j, blocks - 1)
        side_in.append(pl.BlockSpec((None, SIDE_ROWS, cols),
                                    lambda i, j, idx=idx, blk=blk: (idx, blk(i, j), 0)))
        side_out.append(pl.BlockSpec((None, SIDE_ROWS, cols),
                                     lambda i, j, blk=blk: (0, blk(i, j), 0)))
        side_shapes.append(jax.ShapeDtypeStruct((1, rows, cols), BF16))

    outs = pl.pallas_call(
        functools.partial(_norm_up_gate_kernel, tiles_per_seq=seq_len // tm,
                          n_side=len(side_casts)),
        grid=(T // tm, nj),
        in_specs=[
            pl.BlockSpec((tm, D), lambda i, j: (i, 0)),
            pl.BlockSpec((BF16_ROWS, D), lambda i, j: (prev_map(i), 0)),
            pl.BlockSpec((BF16_ROWS, D), lambda i, j: (next_map(i), 0)),
            pl.BlockSpec((1, D), lambda i, j: (0, 0)),
            weight(0), weight(nj), chan(3, 0), chan(3, nj), chan(1, 0), chan(1, nj),
        ] + side_in,
        out_specs=[pl.BlockSpec((tm, tn), lambda i, j: (i, j))] + side_out,
        out_shape=[jax.ShapeDtypeStruct((T, F), BF16)] + side_shapes,
        scratch_shapes=[pltpu.VMEM((tm + 2 * BF16_ROWS, D), BF16)],
        compiler_params=_params(("arbitrary", "arbitrary")),
        name="norm_up_gate",
    )(x2d, x2d, x2d, gain.reshape(1, D), w_up_all, w_up_all, conv_w, conv_w, cb, cb,
      *[w for w, _ in side_casts])
    return outs[0], list(outs[1:])


def _down_residual_kernel(g_ref, w_ref, x_ref, gain_ref, o_ref, *h_refs, norm):
    y = x_ref[...] + jnp.dot(g_ref[...], w_ref[...], preferred_element_type=F32)
    if norm == "final":
        o_ref[...] = _rmsnorm(y, gain_ref[...])
    else:
        o_ref[...] = y
        if norm == "next":
            h_refs[0][...] = _rmsnorm(y, gain_ref[...]).astype(BF16)


def _down_residual(g, w_down, x2d, gain, norm, *, tm):
    T, D = x2d.shape
    F = w_down.shape[0]
    assert T % tm == 0 and norm in (None, "final", "next")
    gain = jnp.ones((D,), F32) if norm is None else gain
    row_spec = pl.BlockSpec((tm, D), lambda i: (i, 0))
    extra = norm == "next"
    outs = pl.pallas_call(
        functools.partial(_down_residual_kernel, norm=norm),
        grid=(T // tm,),
        in_specs=[
            pl.BlockSpec((tm, F), lambda i: (i, 0)),
            pl.BlockSpec((F, D), lambda i: (0, 0), pipeline_mode=pl.Buffered(1)),
            row_spec,
            pl.BlockSpec((1, D), lambda i: (0, 0)),
        ],
        out_specs=[row_spec] + [row_spec] * extra,
        out_shape=[jax.ShapeDtypeStruct((T, D), F32)]
                  + [jax.ShapeDtypeStruct((T, D), BF16)] * extra,
        compiler_params=_params(("parallel",)),
        name="down_residual",
    )(g, w_down, x2d, gain.reshape(1, D))
    return tuple(outs) if extra else outs[0]


def _banded_attention_kernel(slope_ref, q_ref, k_ref, kp_ref, kn_ref, v_ref, vp_ref, vn_ref,
                             acc_ref, m_ref, lse_ref, kbuf, vbuf, bias_ref, slab_ref,
                             mstat_ref, lstat_ref, *, dil, seq, half, qs):
    lq = q_ref.shape[1]
    dh = q_ref.shape[2] // ATTN_HEADS
    n_sb = lq // qs
    it, hg = pl.program_id(1), pl.program_id(2)

    @pl.when(hg == 0)
    def _():
        mstat_ref[...] = jnp.zeros(mstat_ref.shape, F32)
        lstat_ref[...] = jnp.zeros(lstat_ref.shape, F32)

    for buf, prev, main, nxt in ((kbuf, kp_ref, k_ref, kn_ref), (vbuf, vp_ref, v_ref, vn_ref)):
        buf[:, 0:half, :] = prev[...]
        buf[:, half:half + lq, :] = main[...]
        buf[:, half + lq:, :] = nxt[...]

    win = qs + 2 * half
    ii = lax.broadcasted_iota(jnp.int32, (qs, win), 0)
    jj = lax.broadcasted_iota(jnp.int32, (qs, win), 1)
    delta = jnp.abs(jj - half - ii)
    dist = (dil * delta).astype(F32)
    lane = lax.broadcasted_iota(jnp.int32, (qs, LANES), 1)
    head0 = hg * ATTN_HEADS
    own_lanes = (lane >= head0) & (lane < head0 + ATTN_HEADS)
    scale2 = dh ** -0.5 * LOG2E
    neg_slopes2 = [-slope_ref[head0 + hh] * LOG2E for hh in range(ATTN_HEADS)]

    variants = sorted({(sb == 0, sb == n_sb - 1) for sb in range(n_sb)})
    for vi, (is_first, is_last) in enumerate(variants):
        masked = delta > half
        if is_first:
            masked = masked | ((it == 0) & (jj < half))
        if is_last:
            masked = masked | ((it == seq // lq - 1) & (jj >= qs + half))
        for hh in range(ATTN_HEADS):
            bias_ref[vi * ATTN_HEADS + hh] = jnp.where(masked, NEG_INF, neg_slopes2[hh] * dist)

    for sb in range(n_sb):
        vi = variants.index((sb == 0, sb == n_sb - 1))
        for r in range(dil):
            rows = pl.ds(sb * qs, qs) if dil == 1 else pl.ds(r + sb * qs * dil, qs, stride=dil)
            m_tile = jnp.zeros((qs, LANES), F32)
            den_tile = jnp.ones((qs, LANES), F32)
            for hh in range(ATTN_HEADS):
                cols = slice(hh * dh, (hh + 1) * dh)
                q = q_ref[r, sb * qs:(sb + 1) * qs, cols]
                kw = kbuf[r, sb * qs:sb * qs + win, cols]
                vw = vbuf[r, sb * qs:sb * qs + win, cols]
                t = lax.dot_general(q, kw, (((1,), (1,)), ((), ())),
                                    preferred_element_type=F32) * scale2
                t = t + bias_ref[vi * ATTN_HEADS + hh]
                m = jnp.max(t, axis=-1, keepdims=True)
                p = jnp.exp2(t - m)
                den = jnp.sum(p, axis=-1, keepdims=True)
                slab_ref[hh, rows, :] = jnp.dot(p.astype(BF16), vw, preferred_element_type=F32)
                m_tile = jnp.where(lane == head0 + hh, m, m_tile)
                den_tile = jnp.where(lane == head0 + hh, den, den_tile)
            srows = pl.ds(r * lq + sb * qs, qs)
            mstat_ref[srows, :] = jnp.where(own_lanes, m_tile, mstat_ref[srows, :])
            lstat_ref[srows, :] = jnp.where(own_lanes, m_tile + jnp.log2(den_tile),
                                            lstat_ref[srows, :])

    for hh in range(ATTN_HEADS):
        acc_ref[:, hh * dh:(hh + 1) * dh] = slab_ref[hh].astype(acc_ref.dtype)

    @pl.when(hg == pl.num_programs(2) - 1)
    def _():
        for r in range(dil):
            rows = pl.ds(0, lq) if dil == 1 else pl.ds(r, lq, stride=dil)
            m_ref[rows, :] = mstat_ref[r * lq:(r + 1) * lq, :]
            lse_ref[rows, :] = lstat_ref[r * lq:(r + 1) * lq, :]


def _banded_attention(qkv, slopes, *, window, dil):
    B, d, L, D3 = qkv.shape
    D = D3 // 3
    dh = D // N_HEADS
    half = (window // 2) // dil
    qs = LANES
    lq = max(ATTN_TOKENS // dil, qs)
    tokens = lq * dil
    wcols = ATTN_HEADS * dh
    n_hg = N_HEADS // ATTN_HEADS
    assert d == dil and L % lq == 0 and lq % half == 0 and half % BF16_ROWS == 0
    n_t = L // lq
    per = lq // half
    last = L // half - 1

    def main(c):
        return pl.BlockSpec((None, dil, lq, wcols), lambda b, i, g: (b, 0, i, c * n_hg + g))

    def prev(c):
        return pl.BlockSpec((None, dil, half, wcols),
                            lambda b, i, g: (b, 0, jnp.maximum(i * per - 1, 0), c * n_hg + g))

    def nxt(c):
        return pl.BlockSpec((None, dil, half, wcols),
                            lambda b, i, g: (b, 0, jnp.minimum((i + 1) * per, last), c * n_hg + g))

    kern = functools.partial(_banded_attention_kernel, dil=dil, seq=L, half=half, qs=qs)
    T = B * L * dil
    return pl.pallas_call(
        kern,
        grid=(B, n_t, n_hg),
        in_specs=[pl.BlockSpec(memory_space=pltpu.SMEM),
                  main(0), main(1), prev(1), nxt(1), main(2), prev(2), nxt(2)],
        out_specs=[pl.BlockSpec((tokens, wcols), lambda b, i, g: (b * n_t + i, g)),
                   pl.BlockSpec((tokens, LANES), lambda b, i, g: (b * n_t + i, 0)),
                   pl.BlockSpec((tokens, LANES), lambda b, i, g: (b * n_t + i, 0))],
        out_shape=[jax.ShapeDtypeStruct((T, D), BF16),
                   jax.ShapeDtypeStruct((T, LANES), F32),
                   jax.ShapeDtypeStruct((T, LANES), F32)],
        scratch_shapes=[pltpu.VMEM((dil, lq + 2 * half, wcols), BF16),
                        pltpu.VMEM((dil, lq + 2 * half, wcols), BF16),
                        pltpu.VMEM((min(lq // qs, 3) * ATTN_HEADS, qs, qs + 2 * half), F32),
                        pltpu.VMEM((ATTN_HEADS, tokens, dh), F32),
                        pltpu.VMEM((tokens, LANES), F32),
                        pltpu.VMEM((tokens, LANES), F32)],
        compiler_params=_params(("parallel", "parallel", "arbitrary")),
        name="banded_attention",
    )(slopes, qkv, qkv, qkv, qkv, qkv, qkv, qkv)


def _combine_out_kernel(*refs, n_groups):
    acc_refs = refs[:n_groups]
    m_refs = refs[n_groups:2 * n_groups]
    l_refs = refs[2 * n_groups:3 * n_groups]
    x_ref, w_ref, out_ref, mix_ref = refs[3 * n_groups:]
    dh = x_ref.shape[1] // N_HEADS
    lses = [r[...] for r in l_refs]
    top = functools.reduce(jnp.maximum, lses)
    total = functools.reduce(lambda a, b: a + b, [jnp.exp2(l - top) for l in lses])
    lse_all = top + jnp.log2(total)
    coefs = [jnp.exp2(r[...] - lse_all) for r in m_refs]
    for h in range(N_HEADS):
        cols = slice(h * dh, (h + 1) * dh)
        mixed = functools.reduce(
            lambda a, b: a + b,
            [c[:, h:h + 1] * acc[:, cols].astype(F32) for c, acc in zip(coefs, acc_refs)])
        mix_ref[:, cols] = mixed.astype(mix_ref.dtype)
    out_ref[...] = x_ref[...] + jnp.dot(mix_ref[...], w_ref[...],
                                        preferred_element_type=F32)


def _combine_out(accs, maxes, lses, x2d, w_out, *, tm):
    T, D = x2d.shape
    n = len(accs)
    return pl.pallas_call(
        functools.partial(_combine_out_kernel, n_groups=n),
        grid=(T // tm,),
        in_specs=([pl.BlockSpec((tm, D), lambda i: (i, 0))] * n
                  + [pl.BlockSpec((tm, LANES), lambda i: (i, 0))] * (2 * n)
                  + [pl.BlockSpec((tm, D), lambda i: (i, 0)),
                     pl.BlockSpec((D, D), lambda i: (0, 0))]),
        out_specs=pl.BlockSpec((tm, D), lambda i: (i, 0)),
        out_shape=jax.ShapeDtypeStruct((T, D), F32),
        scratch_shapes=[pltpu.VMEM((tm, D), BF16)],
        compiler_params=_params(("parallel",)),
        name="combine_out",
    )(*accs, *maxes, *lses, x2d, w_out)


def kernel(x, mix_norm_g, ffn_norm_g, final_norm_g, sc_w_in, sc_conv_w, sc_conv_b, sc_w_out,
           attn_w_qkv, attn_w_out, ffn_w_up, ffn_conv_w, ffn_conv_b, ffn_w_down):
    B, S, D = x.shape
    T = B * S
    depth = mix_norm_g.shape[0]
    n_mixers = 2
    slopes = jnp.asarray(2.0 ** (-ALIBI_MAX * np.arange(1, N_HEADS + 1) / N_HEADS), dtype=F32)
    xf = x.reshape(T, D)

    def mixer_weight(layer):
        stacked = sc_w_in if layer % n_mixers == 0 else attn_w_qkv
        return stacked, layer // n_mixers

    precast = {}
    h_mix = None
    for i in range(depth):
        j = i // n_mixers
        w_mix, k_mix = precast.get(("mix", i), mixer_weight(i))
        if i % n_mixers == 0:
            y = _norm_in_gate(xf, mix_norm_g[i], w_mix, k_mix, sc_conv_w[j], sc_conv_b[j],
                              seq_len=S, tm=1024, tn=512)
            xf = _down_residual(y, sc_w_out[j].astype(BF16), xf, None, None, tm=512)
        else:
            parts = []
            for g, (window, dil) in enumerate(DILATED_GROUPS):
                qkv = _norm_matmul(xf if h_mix is None else h_mix, mix_norm_g[i], w_mix, k_mix,
                                   batch=B, dil=dil, col0=g * 3 * D, n_cols=3 * D, tm=1024,
                                   tn=2048 if w_mix.dtype == BF16 else 1024)
                parts.append(_banded_attention(qkv, slopes, window=window, dil=dil))
            accs, maxes, lses = zip(*parts)
            xf = _combine_out(accs, maxes, lses, xf, attn_w_out[j].astype(BF16), tm=512)
        w_up, k_up = precast.get(("up", i), (ffn_w_up, i))
        side = [mixer_weight(i + 1), (ffn_w_up, i + 1)] if i + 1 < depth else []
        g, copies = _norm_up_gate(xf, ffn_norm_g[i], w_up, k_up, ffn_conv_w[i], ffn_conv_b[i],
                                  side, seq_len=S, tm=1024, tn=512)
        if side:
            precast[("mix", i + 1)] = (copies[0], 0)
            precast[("up", i + 1)] = (copies[1], 0)
        w_down = ffn_w_down[i].astype(BF16)
        h_mix = None
        if i == depth - 1:
            xf = _down_residual(g, w_down, xf, final_norm_g, "final", tm=512)
        elif (i + 1) % n_mixers == 1:
            xf, h_mix = _down_residual(g, w_down, xf, mix_norm_g[i + 1], "next", tm=512)
        else:
            xf = _down_residual(g, w_down, xf, None, None, tm=512)
    return xf.reshape(B, S, D)
```

```python
import functools

import numpy as np
import jax
import jax.numpy as jnp
from jax import lax
from jax.experimental import pallas as pl
from jax.experimental.pallas import tpu as pltpu

N_HEADS = 16
DILATED_GROUPS = ((128, 1), (512, 4), (2048, 16))
NORM_EPS = 1e-5
ALIBI_MAX = 8.0
NEG_INF = -1e30
LOG2E = 1.4426950408889634

LANES = 128
BF16_ROWS = 16
MXU_DIM = 256
VMEM_LIMIT = 56 * 1024 * 1024
ATTN_TOKENS = 2048
ATTN_HEADS = 4

F32 = jnp.float32
BF16 = jnp.bfloat16


def _params(semantics):
    return pltpu.CompilerParams(dimension_semantics=semantics,
                                vmem_limit_bytes=VMEM_LIMIT)


def _rmsnorm(x, gain):
    ms = jnp.mean(x * x, axis=-1, keepdims=True)
    return x * lax.rsqrt(ms + NORM_EPS) * gain


def _log2(n):
    assert n & (n - 1) == 0
    return n.bit_length() - 1


def _norm_matmul_kernel(x_ref, g_ref, w_ref, o_ref, h_ref, *, dil):
    tm = x_ref.shape[0]
    per = tm // dil

    @pl.when(pl.program_id(1) == 0)
    def _():
        if x_ref.dtype == BF16:
            h = x_ref[...]
        else:
            h = _rmsnorm(x_ref[...], g_ref[...]).astype(BF16)
        if dil == 1:
            h_ref[...] = h
        else:
            n = MXU_DIM // dil
            row = lax.broadcasted_iota(jnp.int32, (MXU_DIM, MXU_DIM), 0)
            col = lax.broadcasted_iota(jnp.int32, (MXU_DIM, MXU_DIM), 1)
            src = (row & (n - 1)) * dil + lax.shift_right_logical(row, _log2(n))
            perm = jnp.where(col == src, 1.0, 0.0).astype(BF16)
            for c in range(tm // MXU_DIM):
                hc = jnp.dot(perm, h[c * MXU_DIM:(c + 1) * MXU_DIM, :],
                             preferred_element_type=F32).astype(BF16)
                for r in range(dil):
                    h_ref[r * per + c * n:r * per + (c + 1) * n, :] = hc[r * n:(r + 1) * n, :]

    res = jnp.dot(h_ref[...], w_ref[...].astype(BF16),
                  preferred_element_type=F32).astype(o_ref.dtype)
    for r in range(dil):
        o_ref[r] = res[r * per:(r + 1) * per, :]


def _norm_matmul(x2d, gain, w_all, layer, *, batch, dil, col0, n_cols, tm, tn):
    T, D = x2d.shape
    S = T // batch
    assert S % tm == 0 and n_cols % tn == 0 and col0 % tn == 0
    assert tm % MXU_DIM == 0 and (MXU_DIM // dil) % BF16_ROWS == 0
    tiles = S // tm
    return pl.pallas_call(
        functools.partial(_norm_matmul_kernel, dil=dil),
        grid=(T // tm, n_cols // tn),
        in_specs=[
            pl.BlockSpec((tm, D), lambda m, j: (m, 0)),
            pl.BlockSpec((1, D), lambda m, j: (0, 0)),
            pl.BlockSpec((None, D, tn), lambda m, j: (layer, 0, col0 // tn + j)),
        ],
        out_specs=pl.BlockSpec((None, dil, tm // dil, tn),
                               lambda m, j: (m // tiles, 0, m % tiles, j)),
        out_shape=jax.ShapeDtypeStruct((batch, dil, S // dil, n_cols), BF16),
        scratch_shapes=[pltpu.VMEM((tm, D), BF16)],
        compiler_params=_params(("parallel", "arbitrary")),
        name="norm_matmul",
    )(x2d, gain.reshape(1, D), w_all)


def _halo_maps(tm, n_rows, rows):
    per = tm // rows
    last = n_rows // rows - 1
    prev_map = lambda i: jnp.maximum(i * per - 1, 0)
    next_map = lambda i: jnp.minimum((i + 1) * per, last)
    return prev_map, next_map


def _fill_halo_norm(x_ref, xp_ref, xn_ref, g_ref, h_ref, tiles_per_seq):
    i = pl.program_id(0)
    tm = x_ref.shape[0]
    halo = BF16_ROWS

    @pl.when(pl.program_id(1) == 0)
    def _():
        first = (i % tiles_per_seq) == 0
        last = ((i + 1) % tiles_per_seq) == 0
        gain = g_ref[...]
        h_ref[0:halo, :] = jnp.where(first, 0.0, _rmsnorm(xp_ref[...], gain)).astype(BF16)
        h_ref[halo:halo + tm, :] = _rmsnorm(x_ref[...], gain).astype(BF16)
        h_ref[halo + tm:, :] = jnp.where(last, 0.0, _rmsnorm(xn_ref[...], gain)).astype(BF16)


def _project(h_ref, w_ref):
    return jnp.dot(h_ref[...], w_ref[...].astype(BF16), preferred_element_type=F32)


def _dwconv3(r, cw_ref, cb_ref):
    rows = r.shape[0]
    return (pltpu.roll(r, 1, 0) * cw_ref[0:1, :] + r * cw_ref[1:2, :]
            + pltpu.roll(r, rows - 1, 0) * cw_ref[2:3, :] + cb_ref[...])


def _side_cast_plan(side_casts, steps, step_index):
    in_specs, out_specs, shapes = [], [], []
    for w_all, idx in side_casts:
        _, rows, cols = w_all.shape
        per = next(r for r in range(BF16_ROWS, rows + 1, BF16_ROWS)
                   if rows % r == 0 and rows // r <= steps)
        blk = lambda i, j, last=rows // per - 1: jnp.minimum(step_index(i, j), last)
        in_specs.append(pl.BlockSpec((None, per, cols),
                                     lambda i, j, idx=idx, blk=blk: (idx, blk(i, j), 0)))
        out_specs.append(pl.BlockSpec((None, per, cols), lambda i, j, blk=blk: (0, blk(i, j), 0)))
        shapes.append(jax.ShapeDtypeStruct((1, rows, cols), BF16))
    return in_specs, out_specs, shapes


def _split_side_refs(refs, n_in, n_side):
    return (refs[:n_in], refs[n_in:n_in + n_side], refs[n_in + n_side],
            refs[n_in + n_side + 1:n_in + 2 * n_side + 1], refs[n_in + 2 * n_side + 1:])


def _norm_in_gate_kernel(*refs, tiles_per_seq, n_side):
    ins, side_src, o_ref, side_dst, (h_ref,) = _split_side_refs(refs, 9, n_side)
    x_ref, xp_ref, xn_ref, g_ref, wu_ref, wb_ref, wc_ref, cw_ref, cb_ref = ins
    _fill_halo_norm(x_ref, xp_ref, xn_ref, g_ref, h_ref, tiles_per_seq)
    for src, dst in zip(side_src, side_dst):
        dst[...] = src[...].astype(dst.dtype)
    tm = x_ref.shape[0]
    v = _project(h_ref, wc_ref) * _project(h_ref, wu_ref)
    y = _project(h_ref, wb_ref) * _dwconv3(v, cw_ref, cb_ref)
    o_ref[...] = y[BF16_ROWS:BF16_ROWS + tm, :].astype(o_ref.dtype)


def _norm_up_gate_kernel(*refs, tiles_per_seq, n_side):
    ins, side_src, o_ref, side_dst, (h_ref,) = _split_side_refs(refs, 10, n_side)
    x_ref, xp_ref, xn_ref, g_ref, wa_ref, wb_ref, cwa_ref, cwb_ref, cba_ref, cbb_ref = ins
    _fill_halo_norm(x_ref, xp_ref, xn_ref, g_ref, h_ref, tiles_per_seq)
    for src, dst in zip(side_src, side_dst):
        dst[...] = src[...].astype(dst.dtype)
    tm = x_ref.shape[0]
    a = _dwconv3(_project(h_ref, wa_ref), cwa_ref, cba_ref)
    b = _dwconv3(_project(h_ref, wb_ref), cwb_ref, cbb_ref)
    gated = (a / (1.0 + jnp.exp(-a))) * b
    o_ref[...] = gated[BF16_ROWS:BF16_ROWS + tm, :].astype(o_ref.dtype)


def _norm_in_gate(x2d, gain, w_in_all, layer, conv_w, conv_b, side_casts, *, seq_len, tm, tn):
    T, D = x2d.shape
    assert seq_len % tm == 0 and D % tn == 0
    nj = D // tn
    prev_map, next_map = _halo_maps(tm, T, BF16_ROWS)
    side_in, side_out, side_shapes = _side_cast_plan(side_casts, (T // tm) * nj,
                                                     lambda i, j: i * nj + j)

    def weight(part):
        return pl.BlockSpec((None, D, tn), lambda i, j: (layer, 0, part * nj + j))

    outs = pl.pallas_call(
        functools.partial(_norm_in_gate_kernel, tiles_per_seq=seq_len // tm,
                          n_side=len(side_casts)),
        grid=(T // tm, nj),
        in_specs=[
            pl.BlockSpec((tm, D), lambda i, j: (i, 0)),
            pl.BlockSpec((BF16_ROWS, D), lambda i, j: (prev_map(i), 0)),
            pl.BlockSpec((BF16_ROWS, D), lambda i, j: (next_map(i), 0)),
            pl.BlockSpec((1, D), lambda i, j: (0, 0)),
            weight(0), weight(1), weight(2),
            pl.BlockSpec((3, tn), lambda i, j: (0, j)),
            pl.BlockSpec((1, tn), lambda i, j: (0, j)),
        ] + side_in,
        out_specs=[pl.BlockSpec((tm, tn), lambda i, j: (i, j))] + side_out,
        out_shape=[jax.ShapeDtypeStruct((T, D), BF16)] + side_shapes,
        scratch_shapes=[pltpu.VMEM((tm + 2 * BF16_ROWS, D), BF16)],
        compiler_params=_params(("arbitrary", "arbitrary")),
        name="norm_in_gate",
    )(x2d, x2d, x2d, gain.reshape(1, D), w_in_all, w_in_all, w_in_all, conv_w,
      conv_b.reshape(1, D), *[w for w, _ in side_casts])
    return outs[0], list(outs[1:])


def _norm_up_gate(x2d, gain, w_up_all, layer, conv_w, conv_b, side_casts, *, seq_len, tm, tn):
    T, D = x2d.shape
    F = w_up_all.shape[2] // 2
    assert seq_len % tm == 0 and F % tn == 0
    nj = F // tn
    prev_map, next_map = _halo_maps(tm, T, BF16_ROWS)
    cb = conv_b.reshape(1, 2 * F)
    side_in, side_out, side_shapes = _side_cast_plan(side_casts, (T // tm) * nj,
                                                     lambda i, j: i * nj + j)

    def chan(rows, off):
        return pl.BlockSpec((rows, tn), lambda i, j: (0, off + j))

    def weight(off):
        return pl.BlockSpec((None, D, tn), lambda i, j: (layer, 0, off + j))

    outs = pl.pallas_call(
        functools.partial(_norm_up_gate_kernel, tiles_per_seq=seq_len // tm,
                          n_side=len(side_casts)),
        grid=(T // tm, nj),
        in_specs=[
            pl.BlockSpec((tm, D), lambda i, j: (i, 0)),
            pl.BlockSpec((BF16_ROWS, D), lambda i, j: (prev_map(i), 0)),
            pl.BlockSpec((BF16_ROWS, D), lambda i, j: (next_map(i), 0)),
            pl.BlockSpec((1, D), lambda i, j: (0, 0)),
            weight(0), weight(nj), chan(3, 0), chan(3, nj), chan(1, 0), chan(1, nj),
        ] + side_in,
        out_specs=[pl.BlockSpec((tm, tn), lambda i, j: (i, j))] + side_out,
        out_shape=[jax.ShapeDtypeStruct((T, F), BF16)] + side_shapes,
        scratch_shapes=[pltpu.VMEM((tm + 2 * BF16_ROWS, D), BF16)],
        compiler_params=_params(("arbitrary", "arbitrary")),
        name="norm_up_gate",
    )(x2d, x2d, x2d, gain.reshape(1, D), w_up_all, w_up_all, conv_w, conv_w, cb, cb,
      *[w for w, _ in side_casts])
    return outs[0], list(outs[1:])


def _down_residual_kernel(g_ref, w_ref, x_ref, gain_ref, o_ref, *h_refs, norm):
    y = x_ref[...] + jnp.dot(g_ref[...], w_ref[...], preferred_element_type=F32)
    if norm == "final":
        o_ref[...] = _rmsnorm(y, gain_ref[...])
    else:
        o_ref[...] = y
        if norm == "next":
            h_refs[0][...] = _rmsnorm(y, gain_ref[...]).astype(BF16)


def _down_residual(g, w_down, x2d, gain, norm, *, tm):
    T, D = x2d.shape
    F = w_down.shape[0]
    assert T % tm == 0 and norm in (None, "final", "next")
    gain = jnp.ones((D,), F32) if norm is None else gain
    row_spec = pl.BlockSpec((tm, D), lambda i: (i, 0))
    extra = norm == "next"
    outs = pl.pallas_call(
        functools.partial(_down_residual_kernel, norm=norm),
        grid=(T // tm,),
        in_specs=[
            pl.BlockSpec((tm, F), lambda i: (i, 0)),
            pl.BlockSpec((F, D), lambda i: (0, 0), pipeline_mode=pl.Buffered(1)),
            row_spec,
            pl.BlockSpec((1, D), lambda i: (0, 0)),
        ],
        out_specs=[row_spec] + [row_spec] * extra,
        out_shape=[jax.ShapeDtypeStruct((T, D), F32)]
                  + [jax.ShapeDtypeStruct((T, D), BF16)] * extra,
        compiler_params=_params(("parallel",)),
        name="down_residual",
    )(g, w_down, x2d, gain.reshape(1, D))
    return tuple(outs) if extra else outs[0]


def _banded_attention_kernel(slope_ref, q_ref, k_ref, kp_ref, kn_ref, v_ref, vp_ref, vn_ref,
                             acc_ref, m_ref, lse_ref, kbuf, vbuf, bias_ref, slab_ref,
                             mstat_ref, lstat_ref, *, dil, seq, half, qs):
    lq = q_ref.shape[1]
    dh = q_ref.shape[2] // ATTN_HEADS
    n_sb = lq // qs
    it, hg = pl.program_id(1), pl.program_id(2)

    @pl.when(hg == 0)
    def _():
        mstat_ref[...] = jnp.zeros(mstat_ref.shape, F32)
        lstat_ref[...] = jnp.zeros(lstat_ref.shape, F32)

    for buf, prev, main, nxt in ((kbuf, kp_ref, k_ref, kn_ref), (vbuf, vp_ref, v_ref, vn_ref)):
        buf[:, 0:half, :] = prev[...]
        buf[:, half:half + lq, :] = main[...]
        buf[:, half + lq:, :] = nxt[...]

    win = qs + 2 * half
    ii = lax.broadcasted_iota(jnp.int32, (qs, win), 0)
    jj = lax.broadcasted_iota(jnp.int32, (qs, win), 1)
    delta = jnp.abs(jj - half - ii)
    dist = (dil * delta).astype(F32)
    lane = lax.broadcasted_iota(jnp.int32, (qs, LANES), 1)
    head0 = hg * ATTN_HEADS
    own_lanes = (lane >= head0) & (lane < head0 + ATTN_HEADS)
    scale2 = dh ** -0.5 * LOG2E
    neg_slopes2 = [-slope_ref[head0 + hh] * LOG2E for hh in range(ATTN_HEADS)]

    variants = sorted({(sb == 0, sb == n_sb - 1) for sb in range(n_sb)})
    for vi, (is_first, is_last) in enumerate(variants):
        masked = delta > half
        if is_first:
            masked = masked | ((it == 0) & (jj < half))
        if is_last:
            masked = masked | ((it == seq // lq - 1) & (jj >= qs + half))
        for hh in range(ATTN_HEADS):
            bias_ref[vi * ATTN_HEADS + hh] = jnp.where(masked, NEG_INF, neg_slopes2[hh] * dist)

    for sb in range(n_sb):
        vi = variants.index((sb == 0, sb == n_sb - 1))
        for r in range(dil):
            rows = pl.ds(sb * qs, qs) if dil == 1 else pl.ds(r + sb * qs * dil, qs, stride=dil)
            m_tile = jnp.zeros((qs, LANES), F32)
            den_tile = jnp.ones((qs, LANES), F32)
            for hh in range(ATTN_HEADS):
                cols = slice(hh * dh, (hh + 1) * dh)
                q = q_ref[r, sb * qs:(sb + 1) * qs, cols]
                kw = kbuf[r, sb * qs:sb * qs + win, cols]
                vw = vbuf[r, sb * qs:sb * qs + win, cols]
                t = lax.dot_general(q, kw, (((1,), (1,)), ((), ())),
                                    preferred_element_type=F32) * scale2
                t = t + bias_ref[vi * ATTN_HEADS + hh]
                m = jnp.max(t, axis=-1, keepdims=True)
                p = jnp.exp2(t - m)
                den = jnp.sum(p, axis=-1, keepdims=True)
                slab_ref[hh, rows, :] = jnp.dot(p.astype(BF16), vw, preferred_element_type=F32)
                m_tile = jnp.where(lane == head0 + hh, m, m_tile)
                den_tile = jnp.where(lane == head0 + hh, den, den_tile)
            srows = pl.ds(r * lq + sb * qs, qs)
            mstat_ref[srows, :] = jnp.where(own_lanes, m_tile, mstat_ref[srows, :])
            lstat_ref[srows, :] = jnp.where(own_lanes, m_tile + jnp.log2(den_tile),
                                            lstat_ref[srows, :])

    for hh in range(ATTN_HEADS):
        acc_ref[:, hh * dh:(hh + 1) * dh] = slab_ref[hh].astype(acc_ref.dtype)

    @pl.when(hg == pl.num_programs(2) - 1)
    def _():
        for r in range(dil):
            rows = pl.ds(0, lq) if dil == 1 else pl.ds(r, lq, stride=dil)
            m_ref[rows, :] = mstat_ref[r * lq:(r + 1) * lq, :]
            lse_ref[rows, :] = lstat_ref[r * lq:(r + 1) * lq, :]


def _banded_attention(qkv, slopes, *, window, dil):
    B, d, L, D3 = qkv.shape
    D = D3 // 3
    dh = D // N_HEADS
    half = (window // 2) // dil
    qs = LANES
    lq = max(ATTN_TOKENS // dil, qs)
    tokens = lq * dil
    wcols = ATTN_HEADS * dh
    n_hg = N_HEADS // ATTN_HEADS
    assert d == dil and L % lq == 0 and lq % half == 0 and half % BF16_ROWS == 0
    n_t = L // lq
    per = lq // half
    last = L // half - 1

    def main(c):
        return pl.BlockSpec((None, dil, lq, wcols), lambda b, i, g: (b, 0, i, c * n_hg + g))

    def prev(c):
        return pl.BlockSpec((None, dil, half, wcols),
                            lambda b, i, g: (b, 0, jnp.maximum(i * per - 1, 0), c * n_hg + g))

    def nxt(c):
        return pl.BlockSpec((None, dil, half, wcols),
                            lambda b, i, g: (b, 0, jnp.minimum((i + 1) * per, last), c * n_hg + g))

    kern = functools.partial(_banded_attention_kernel, dil=dil, seq=L, half=half, qs=qs)
    T = B * L * dil
    return pl.pallas_call(
        kern,
        grid=(B, n_t, n_hg),
        in_specs=[pl.BlockSpec(memory_space=pltpu.SMEM),
                  main(0), main(1), prev(1), nxt(1), main(2), prev(2), nxt(2)],
        out_specs=[pl.BlockSpec((tokens, wcols), lambda b, i, g: (b * n_t + i, g)),
                   pl.BlockSpec((tokens, LANES), lambda b, i, g: (b * n_t + i, 0)),
                   pl.BlockSpec((tokens, LANES), lambda b, i, g: (b * n_t + i, 0))],
        out_shape=[jax.ShapeDtypeStruct((T, D), BF16),
                   jax.ShapeDtypeStruct((T, LANES), F32),
                   jax.ShapeDtypeStruct((T, LANES), F32)],
        scratch_shapes=[pltpu.VMEM((dil, lq + 2 * half, wcols), BF16),
                        pltpu.VMEM((dil, lq + 2 * half, wcols), BF16),
                        pltpu.VMEM((min(lq // qs, 3) * ATTN_HEADS, qs, qs + 2 * half), F32),
                        pltpu.VMEM((ATTN_HEADS, tokens, dh), F32),
                        pltpu.VMEM((tokens, LANES), F32),
                        pltpu.VMEM((tokens, LANES), F32)],
        compiler_params=_params(("parallel", "parallel", "arbitrary")),
        name="banded_attention",
    )(slopes, qkv, qkv, qkv, qkv, qkv, qkv, qkv)


def _combine_out_kernel(*refs, n_groups):
    acc_refs = refs[:n_groups]
    m_refs = refs[n_groups:2 * n_groups]
    l_refs = refs[2 * n_groups:3 * n_groups]
    x_ref, w_ref, out_ref, mix_ref = refs[3 * n_groups:]
    dh = x_ref.shape[1] // N_HEADS
    lses = [r[...] for r in l_refs]
    top = functools.reduce(jnp.maximum, lses)
    total = functools.reduce(lambda a, b: a + b, [jnp.exp2(l - top) for l in lses])
    lse_all = top + jnp.log2(total)
    coefs = [jnp.exp2(r[...] - lse_all) for r in m_refs]
    for h in range(N_HEADS):
        cols = slice(h * dh, (h + 1) * dh)
        mixed = functools.reduce(
            lambda a, b: a + b,
            [c[:, h:h + 1] * acc[:, cols].astype(F32) for c, acc in zip(coefs, acc_refs)])
        mix_ref[:, cols] = mixed.astype(mix_ref.dtype)
    out_ref[...] = x_ref[...] + jnp.dot(mix_ref[...], w_ref[...],
                                        preferred_element_type=F32)


def _combine_out(accs, maxes, lses, x2d, w_out, *, tm):
    T, D = x2d.shape
    n = len(accs)
    return pl.pallas_call(
        functools.partial(_combine_out_kernel, n_groups=n),
        grid=(T // tm,),
        in_specs=([pl.BlockSpec((tm, D), lambda i: (i, 0))] * n
                  + [pl.BlockSpec((tm, LANES), lambda i: (i, 0))] * (2 * n)
                  + [pl.BlockSpec((tm, D), lambda i: (i, 0)),
                     pl.BlockSpec((D, D), lambda i: (0, 0))]),
        out_specs=pl.BlockSpec((tm, D), lambda i: (i, 0)),
        out_shape=jax.ShapeDtypeStruct((T, D), F32),
        scratch_shapes=[pltpu.VMEM((tm, D), BF16)],
        compiler_params=_params(("parallel",)),
        name="combine_out",
    )(*accs, *maxes, *lses, x2d, w_out)


def kernel(x, mix_norm_g, ffn_norm_g, final_norm_g, sc_w_in, sc_conv_w, sc_conv_b, sc_w_out,
           attn_w_qkv, attn_w_out, ffn_w_up, ffn_conv_w, ffn_conv_b, ffn_w_down):
    B, S, D = x.shape
    T = B * S
    depth = mix_norm_g.shape[0]
    n_mixers = 2
    slopes = jnp.asarray(2.0 ** (-ALIBI_MAX * np.arange(1, N_HEADS + 1) / N_HEADS), dtype=F32)
    xf = x.reshape(T, D)

    def source(kind, layer):
        if kind == "mix":
            return (sc_w_in if layer % n_mixers == 0 else attn_w_qkv), layer // n_mixers
        if kind == "out":
            return (sc_w_out if layer % n_mixers == 0 else attn_w_out), layer // n_mixers
        return (ffn_w_up if kind == "up" else ffn_w_down), layer

    cast = {}

    def weight(kind, layer):
        return cast.get((kind, layer), source(kind, layer))

    def resident(kind, layer):
        w_all, idx = weight(kind, layer)
        return w_all[idx].astype(BF16)

    def run_with_side_casts(fn, wanted):
        todo = [key for key in wanted if key not in cast]
        out, copies = fn([source(*key) for key in todo])
        cast.update({key: (copy, 0) for key, copy in zip(todo, copies)})
        return out

    h_mix = None
    for i in range(depth):
        j = i // n_mixers
        w_mix, k_mix = weight("mix", i)
        if i % n_mixers == 0:
            y = run_with_side_casts(
                lambda side: _norm_in_gate(xf, mix_norm_g[i], w_mix, k_mix, sc_conv_w[j],
                                           sc_conv_b[j], side, seq_len=S, tm=1024, tn=512),
                [("out", i)])
            xf = _down_residual(y, resident("out", i), xf, None, None, tm=512)
        else:
            parts = []
            for g, (window, dil) in enumerate(DILATED_GROUPS):
                qkv = _norm_matmul(xf if h_mix is None else h_mix, mix_norm_g[i], w_mix, k_mix,
                                   batch=B, dil=dil, col0=g * 3 * D, n_cols=3 * D, tm=1024,
                                   tn=2048 if w_mix.dtype == BF16 else 1024)
                parts.append(_banded_attention(qkv, slopes, window=window, dil=dil))
            accs, maxes, lses = zip(*parts)
            xf = _combine_out(accs, maxes, lses, xf, resident("out", i), tm=512)
        w_up, k_up = weight("up", i)
        later = [(kind, i + 1) for kind in ("mix", "up", "out", "down")] if i + 1 < depth else []
        g = run_with_side_casts(
            lambda side: _norm_up_gate(xf, ffn_norm_g[i], w_up, k_up, ffn_conv_w[i],
                                       ffn_conv_b[i], side, seq_len=S, tm=1024, tn=512),
            later)
        w_down = resident("down", i)
        h_mix = None
        if i == depth - 1:
            xf = _down_residual(g, w_down, xf, final_norm_g, "final", tm=512)
        elif (i + 1) % n_mixers == 1:
            xf, h_mix = _down_residual(g, w_down, xf, mix_norm_g[i + 1], "next", tm=512)
        else:
            xf = _down_residual(g, w_down, xf, None, None, tm=512)
    return xf.reshape(B, S, D)
```

```python
import functools

import numpy as np
import jax
import jax.numpy as jnp
from jax import lax
from jax.experimental import pallas as pl
from jax.experimental.pallas import tpu as pltpu

N_HEADS = 16
DILATED_GROUPS = ((128, 1), (512, 4), (2048, 16))
NORM_EPS = 1e-5
ALIBI_MAX = 8.0
NEG_INF = -1e30
LOG2E = 1.4426950408889634

LANES = 128
BF16_ROWS = 16
MXU_DIM = 256
VMEM_LIMIT = 56 * 1024 * 1024
ATTN_TOKENS = 2048
ATTN_HEADS = 4

F32 = jnp.float32
BF16 = jnp.bfloat16


def _params(semantics):
    return pltpu.CompilerParams(dimension_semantics=semantics,
                                vmem_limit_bytes=VMEM_LIMIT)


def _rmsnorm(x, gain):
    ms = jnp.mean(x * x, axis=-1, keepdims=True)
    return x * lax.rsqrt(ms + NORM_EPS) * gain


def _log2(n):
    assert n & (n - 1) == 0
    return n.bit_length() - 1


def _norm_matmul_kernel(x_ref, g_ref, w_ref, o_ref, h_ref, *, dil):
    tm = x_ref.shape[0]
    per = tm // dil

    @pl.when(pl.program_id(1) == 0)
    def _():
        if x_ref.dtype == BF16:
            h = x_ref[...]
        else:
            h = _rmsnorm(x_ref[...], g_ref[...]).astype(BF16)
        if dil == 1:
            h_ref[...] = h
        else:
            n = MXU_DIM // dil
            row = lax.broadcasted_iota(jnp.int32, (MXU_DIM, MXU_DIM), 0)
            col = lax.broadcasted_iota(jnp.int32, (MXU_DIM, MXU_DIM), 1)
            src = (row & (n - 1)) * dil + lax.shift_right_logical(row, _log2(n))
            perm = jnp.where(col == src, 1.0, 0.0).astype(BF16)
            for c in range(tm // MXU_DIM):
                hc = jnp.dot(perm, h[c * MXU_DIM:(c + 1) * MXU_DIM, :],
                             preferred_element_type=F32).astype(BF16)
                for r in range(dil):
                    h_ref[r * per + c * n:r * per + (c + 1) * n, :] = hc[r * n:(r + 1) * n, :]

    res = jnp.dot(h_ref[...], w_ref[...].astype(BF16),
                  preferred_element_type=F32).astype(o_ref.dtype)
    for r in range(dil):
        o_ref[r] = res[r * per:(r + 1) * per, :]


def _norm_matmul(x2d, gain, w_all, layer, *, batch, dil, col0, n_cols, tm, tn):
    T, D = x2d.shape
    S = T // batch
    assert S % tm == 0 and n_cols % tn == 0 and col0 % tn == 0
    assert tm % MXU_DIM == 0 and (MXU_DIM // dil) % BF16_ROWS == 0
    tiles = S // tm
    return pl.pallas_call(
        functools.partial(_norm_matmul_kernel, dil=dil),
        grid=(T // tm, n_cols // tn),
        in_specs=[
            pl.BlockSpec((tm, D), lambda m, j: (m, 0)),
            pl.BlockSpec((1, D), lambda m, j: (0, 0)),
            pl.BlockSpec((None, D, tn), lambda m, j: (layer, 0, col0 // tn + j)),
        ],
        out_specs=pl.BlockSpec((None, dil, tm // dil, tn),
                               lambda m, j: (m // tiles, 0, m % tiles, j)),
        out_shape=jax.ShapeDtypeStruct((batch, dil, S // dil, n_cols), BF16),
        scratch_shapes=[pltpu.VMEM((tm, D), BF16)],
        compiler_params=_params(("parallel", "arbitrary")),
        name="norm_matmul",
    )(x2d, gain.reshape(1, D), w_all)


def _halo_maps(tm, n_rows, rows):
    per = tm // rows
    last = n_rows // rows - 1
    prev_map = lambda i: jnp.maximum(i * per - 1, 0)
    next_map = lambda i: jnp.minimum((i + 1) * per, last)
    return prev_map, next_map


def _fill_halo_norm(x_ref, xp_ref, xn_ref, g_ref, h_ref, tiles_per_seq):
    i = pl.program_id(0)
    tm = x_ref.shape[0]
    halo = BF16_ROWS

    @pl.when(pl.program_id(1) == 0)
    def _():
        first = (i % tiles_per_seq) == 0
        last = ((i + 1) % tiles_per_seq) == 0
        gain = g_ref[...]
        h_ref[0:halo, :] = jnp.where(first, 0.0, _rmsnorm(xp_ref[...], gain)).astype(BF16)
        h_ref[halo:halo + tm, :] = _rmsnorm(x_ref[...], gain).astype(BF16)
        h_ref[halo + tm:, :] = jnp.where(last, 0.0, _rmsnorm(xn_ref[...], gain)).astype(BF16)


def _project(h_ref, w_ref):
    return jnp.dot(h_ref[...], w_ref[...].astype(BF16), preferred_element_type=F32)


def _dwconv3(r, cw_ref, cb_ref):
    rows = r.shape[0]
    return (pltpu.roll(r, 1, 0) * cw_ref[0:1, :] + r * cw_ref[1:2, :]
            + pltpu.roll(r, rows - 1, 0) * cw_ref[2:3, :] + cb_ref[...])


def _side_cast_plan(side_casts, steps, step_index):
    in_specs, out_specs, shapes = [], [], []
    for w_all, idx in side_casts:
        _, rows, cols = w_all.shape
        per = next(r for r in range(BF16_ROWS, rows + 1, BF16_ROWS)
                   if rows % r == 0 and rows // r <= steps)
        blk = lambda *ids, last=rows // per - 1: jnp.minimum(step_index(*ids), last)
        in_specs.append(pl.BlockSpec((None, per, cols),
                                     lambda *ids, idx=idx, blk=blk: (idx, blk(*ids), 0)))
        out_specs.append(pl.BlockSpec((None, per, cols), lambda *ids, blk=blk: (0, blk(*ids), 0)))
        shapes.append(jax.ShapeDtypeStruct((1, rows, cols), BF16))
    return in_specs, out_specs, shapes


def _split_side_refs(refs, n_in, n_side):
    return (refs[:n_in], refs[n_in:n_in + n_side], refs[n_in + n_side],
            refs[n_in + n_side + 1:n_in + 2 * n_side + 1], refs[n_in + 2 * n_side + 1:])


def _norm_in_gate_kernel(*refs, tiles_per_seq, n_side):
    ins, side_src, o_ref, side_dst, (h_ref,) = _split_side_refs(refs, 9, n_side)
    x_ref, xp_ref, xn_ref, g_ref, wu_ref, wb_ref, wc_ref, cw_ref, cb_ref = ins
    _fill_halo_norm(x_ref, xp_ref, xn_ref, g_ref, h_ref, tiles_per_seq)
    for src, dst in zip(side_src, side_dst):
        dst[...] = src[...].astype(dst.dtype)
    tm = x_ref.shape[0]
    v = _project(h_ref, wc_ref) * _project(h_ref, wu_ref)
    y = _project(h_ref, wb_ref) * _dwconv3(v, cw_ref, cb_ref)
    o_ref[...] = y[BF16_ROWS:BF16_ROWS + tm, :].astype(o_ref.dtype)


def _norm_up_gate_kernel(*refs, tiles_per_seq, n_side):
    ins, side_src, o_ref, side_dst, (h_ref,) = _split_side_refs(refs, 10, n_side)
    x_ref, xp_ref, xn_ref, g_ref, wa_ref, wb_ref, cwa_ref, cwb_ref, cba_ref, cbb_ref = ins
    _fill_halo_norm(x_ref, xp_ref, xn_ref, g_ref, h_ref, tiles_per_seq)
    for src, dst in zip(side_src, side_dst):
        dst[...] = src[...].astype(dst.dtype)
    tm = x_ref.shape[0]
    a = _dwconv3(_project(h_ref, wa_ref), cwa_ref, cba_ref)
    b = _dwconv3(_project(h_ref, wb_ref), cwb_ref, cbb_ref)
    gated = (a / (1.0 + jnp.exp(-a))) * b
    o_ref[...] = gated[BF16_ROWS:BF16_ROWS + tm, :].astype(o_ref.dtype)


def _norm_in_gate(x2d, gain, w_in_all, layer, conv_w, conv_b, side_casts, *, seq_len, tm, tn):
    T, D = x2d.shape
    assert seq_len % tm == 0 and D % tn == 0
    nj = D // tn
    prev_map, next_map = _halo_maps(tm, T, BF16_ROWS)
    side_in, side_out, side_shapes = _side_cast_plan(side_casts, (T // tm) * nj,
                                                     lambda i, j: i * nj + j)

    def weight(part):
        return pl.BlockSpec((None, D, tn), lambda i, j: (layer, 0, part * nj + j))

    outs = pl.pallas_call(
        functools.partial(_norm_in_gate_kernel, tiles_per_seq=seq_len // tm,
                          n_side=len(side_casts)),
        grid=(T // tm, nj),
        in_specs=[
            pl.BlockSpec((tm, D), lambda i, j: (i, 0)),
            pl.BlockSpec((BF16_ROWS, D), lambda i, j: (prev_map(i), 0)),
            pl.BlockSpec((BF16_ROWS, D), lambda i, j: (next_map(i), 0)),
            pl.BlockSpec((1, D), lambda i, j: (0, 0)),
            weight(0), weight(1), weight(2),
            pl.BlockSpec((3, tn), lambda i, j: (0, j)),
            pl.BlockSpec((1, tn), lambda i, j: (0, j)),
        ] + side_in,
        out_specs=[pl.BlockSpec((tm, tn), lambda i, j: (i, j))] + side_out,
        out_shape=[jax.ShapeDtypeStruct((T, D), BF16)] + side_shapes,
        scratch_shapes=[pltpu.VMEM((tm + 2 * BF16_ROWS, D), BF16)],
        compiler_params=_params(("arbitrary", "arbitrary")),
        name="norm_in_gate",
    )(x2d, x2d, x2d, gain.reshape(1, D), w_in_all, w_in_all, w_in_all, conv_w,
      conv_b.reshape(1, D), *[w for w, _ in side_casts])
    return outs[0], list(outs[1:])


def _norm_up_gate(x2d, gain, w_up_all, layer, conv_w, conv_b, side_casts, *, seq_len, tm, tn):
    T, D = x2d.shape
    F = w_up_all.shape[2] // 2
    assert seq_len % tm == 0 and F % tn == 0
    nj = F // tn
    prev_map, next_map = _halo_maps(tm, T, BF16_ROWS)
    cb = conv_b.reshape(1, 2 * F)
    side_in, side_out, side_shapes = _side_cast_plan(side_casts, (T // tm) * nj,
                                                     lambda i, j: i * nj + j)

    def chan(rows, off):
        return pl.BlockSpec((rows, tn), lambda i, j: (0, off + j))

    def weight(off):
        return pl.BlockSpec((None, D, tn), lambda i, j: (layer, 0, off + j))

    outs = pl.pallas_call(
        functools.partial(_norm_up_gate_kernel, tiles_per_seq=seq_len // tm,
                          n_side=len(side_casts)),
        grid=(T // tm, nj),
        in_specs=[
            pl.BlockSpec((tm, D), lambda i, j: (i, 0)),
            pl.BlockSpec((BF16_ROWS, D), lambda i, j: (prev_map(i), 0)),
            pl.BlockSpec((BF16_ROWS, D), lambda i, j: (next_map(i), 0)),
            pl.BlockSpec((1, D), lambda i, j: (0, 0)),
            weight(0), weight(nj), chan(3, 0), chan(3, nj), chan(1, 0), chan(1, nj),
        ] + side_in,
        out_specs=[pl.BlockSpec((tm, tn), lambda i, j: (i, j))] + side_out,
        out_shape=[jax.ShapeDtypeStruct((T, F), BF16)] + side_shapes,
        scratch_shapes=[pltpu.VMEM((tm + 2 * BF16_ROWS, D), BF16)],
        compiler_params=_params(("arbitrary", "arbitrary")),
        name="norm_up_gate",
    )(x2d, x2d, x2d, gain.reshape(1, D), w_up_all, w_up_all, conv_w, conv_w, cb, cb,
      *[w for w, _ in side_casts])
    return outs[0], list(outs[1:])


def _down_residual_kernel(*refs, norm, n_side):
    g_ref, w_ref, x_ref, gain_ref = refs[:4]
    side_src = refs[4:4 + n_side]
    outs = refs[4 + n_side:]
    o_ref = outs[0]
    side_dst = outs[len(outs) - n_side:]
    for src, dst in zip(side_src, side_dst):
        dst[...] = src[...].astype(dst.dtype)
    y = x_ref[...] + jnp.dot(g_ref[...], w_ref[...], preferred_element_type=F32)
    if norm == "final":
        o_ref[...] = _rmsnorm(y, gain_ref[...])
    else:
        o_ref[...] = y
        if norm == "next":
            outs[1][...] = _rmsnorm(y, gain_ref[...]).astype(BF16)


def _down_residual(g, w_down, x2d, gain, norm, side_casts, *, tm):
    T, D = x2d.shape
    F = w_down.shape[0]
    assert T % tm == 0 and norm in (None, "final", "next")
    gain = jnp.ones((D,), F32) if norm is None else gain
    row_spec = pl.BlockSpec((tm, D), lambda i: (i, 0))
    n_main = 2 if norm == "next" else 1
    side_in, side_out, side_shapes = _side_cast_plan(side_casts, T // tm, lambda i: i)
    outs = pl.pallas_call(
        functools.partial(_down_residual_kernel, norm=norm, n_side=len(side_casts)),
        grid=(T // tm,),
        in_specs=[
            pl.BlockSpec((tm, F), lambda i: (i, 0)),
            pl.BlockSpec((F, D), lambda i: (0, 0), pipeline_mode=pl.Buffered(1)),
            row_spec,
            pl.BlockSpec((1, D), lambda i: (0, 0)),
        ] + side_in,
        out_specs=[row_spec] * n_main + side_out,
        out_shape=[jax.ShapeDtypeStruct((T, D), F32)]
                  + [jax.ShapeDtypeStruct((T, D), BF16)] * (n_main - 1) + side_shapes,
        compiler_params=_params(("arbitrary",)),
        name="down_residual",
    )(g, w_down, x2d, gain.reshape(1, D), *[w for w, _ in side_casts])
    main = tuple(outs[:n_main]) if n_main > 1 else outs[0]
    return main, list(outs[n_main:])


def _banded_attention_kernel(slope_ref, q_ref, k_ref, kp_ref, kn_ref, v_ref, vp_ref, vn_ref,
                             acc_ref, m_ref, lse_ref, kbuf, vbuf, bias_ref, slab_ref,
                             mstat_ref, lstat_ref, *, dil, seq, half, qs):
    lq = q_ref.shape[1]
    dh = q_ref.shape[2] // ATTN_HEADS
    n_sb = lq // qs
    it, hg = pl.program_id(1), pl.program_id(2)

    @pl.when(hg == 0)
    def _():
        mstat_ref[...] = jnp.zeros(mstat_ref.shape, F32)
        lstat_ref[...] = jnp.zeros(lstat_ref.shape, F32)

    for buf, prev, main, nxt in ((kbuf, kp_ref, k_ref, kn_ref), (vbuf, vp_ref, v_ref, vn_ref)):
        buf[:, 0:half, :] = prev[...]
        buf[:, half:half + lq, :] = main[...]
        buf[:, half + lq:, :] = nxt[...]

    win = qs + 2 * half
    ii = lax.broadcasted_iota(jnp.int32, (qs, win), 0)
    jj = lax.broadcasted_iota(jnp.int32, (qs, win), 1)
    delta = jnp.abs(jj - half - ii)
    dist = (dil * delta).astype(F32)
    lane = lax.broadcasted_iota(jnp.int32, (qs, LANES), 1)
    head0 = hg * ATTN_HEADS
    own_lanes = (lane >= head0) & (lane < head0 + ATTN_HEADS)
    scale2 = dh ** -0.5 * LOG2E
    neg_slopes2 = [-slope_ref[head0 + hh] * LOG2E for hh in range(ATTN_HEADS)]

    variants = sorted({(sb == 0, sb == n_sb - 1) for sb in range(n_sb)})
    for vi, (is_first, is_last) in enumerate(variants):
        masked = delta > half
        if is_first:
            masked = masked | ((it == 0) & (jj < half))
        if is_last:
            masked = masked | ((it == seq // lq - 1) & (jj >= qs + half))
        for hh in range(ATTN_HEADS):
            bias_ref[vi * ATTN_HEADS + hh] = jnp.where(masked, NEG_INF, neg_slopes2[hh] * dist)

    for sb in range(n_sb):
        vi = variants.index((sb == 0, sb == n_sb - 1))
        for r in range(dil):
            rows = pl.ds(sb * qs, qs) if dil == 1 else pl.ds(r + sb * qs * dil, qs, stride=dil)
            m_tile = jnp.zeros((qs, LANES), F32)
            den_tile = jnp.ones((qs, LANES), F32)
            for hh in range(ATTN_HEADS):
                cols = slice(hh * dh, (hh + 1) * dh)
                q = q_ref[r, sb * qs:(sb + 1) * qs, cols]
                kw = kbuf[r, sb * qs:sb * qs + win, cols]
                vw = vbuf[r, sb * qs:sb * qs + win, cols]
                t = lax.dot_general(q, kw, (((1,), (1,)), ((), ())),
                                    preferred_element_type=F32) * scale2
                t = t + bias_ref[vi * ATTN_HEADS + hh]
                m = jnp.max(t, axis=-1, keepdims=True)
                p = jnp.exp2(t - m)
                den = jnp.sum(p, axis=-1, keepdims=True)
                slab_ref[hh, rows, :] = jnp.dot(p.astype(BF16), vw, preferred_element_type=F32)
                m_tile = jnp.where(lane == head0 + hh, m, m_tile)
                den_tile = jnp.where(lane == head0 + hh, den, den_tile)
            srows = pl.ds(r * lq + sb * qs, qs)
            mstat_ref[srows, :] = jnp.where(own_lanes, m_tile, mstat_ref[srows, :])
            lstat_ref[srows, :] = jnp.where(own_lanes, m_tile + jnp.log2(den_tile),
                                            lstat_ref[srows, :])

    for hh in range(ATTN_HEADS):
        acc_ref[:, hh * dh:(hh + 1) * dh] = slab_ref[hh].astype(acc_ref.dtype)

    @pl.when(hg == pl.num_programs(2) - 1)
    def _():
        for r in range(dil):
            rows = pl.ds(0, lq) if dil == 1 else pl.ds(r, lq, stride=dil)
            m_ref[rows, :] = mstat_ref[r * lq:(r + 1) * lq, :]
            lse_ref[rows, :] = lstat_ref[r * lq:(r + 1) * lq, :]


def _banded_attention(qkv, slopes, *, window, dil):
    B, d, L, D3 = qkv.shape
    D = D3 // 3
    dh = D // N_HEADS
    half = (window // 2) // dil
    qs = LANES
    lq = max(ATTN_TOKENS // dil, qs)
    tokens = lq * dil
    wcols = ATTN_HEADS * dh
    n_hg = N_HEADS // ATTN_HEADS
    assert d == dil and L % lq == 0 and lq % half == 0 and half % BF16_ROWS == 0
    n_t = L // lq
    per = lq // half
    last = L // half - 1

    def main(c):
        return pl.BlockSpec((None, dil, lq, wcols), lambda b, i, g: (b, 0, i, c * n_hg + g))

    def prev(c):
        return pl.BlockSpec((None, dil, half, wcols),
                            lambda b, i, g: (b, 0, jnp.maximum(i * per - 1, 0), c * n_hg + g))

    def nxt(c):
        return pl.BlockSpec((None, dil, half, wcols),
                            lambda b, i, g: (b, 0, jnp.minimum((i + 1) * per, last), c * n_hg + g))

    kern = functools.partial(_banded_attention_kernel, dil=dil, seq=L, half=half, qs=qs)
    T = B * L * dil
    return pl.pallas_call(
        kern,
        grid=(B, n_t, n_hg),
        in_specs=[pl.BlockSpec(memory_space=pltpu.SMEM),
                  main(0), main(1), prev(1), nxt(1), main(2), prev(2), nxt(2)],
        out_specs=[pl.BlockSpec((tokens, wcols), lambda b, i, g: (b * n_t + i, g)),
                   pl.BlockSpec((tokens, LANES), lambda b, i, g: (b * n_t + i, 0)),
                   pl.BlockSpec((tokens, LANES), lambda b, i, g: (b * n_t + i, 0))],
        out_shape=[jax.ShapeDtypeStruct((T, D), BF16),
                   jax.ShapeDtypeStruct((T, LANES), F32),
                   jax.ShapeDtypeStruct((T, LANES), F32)],
        scratch_shapes=[pltpu.VMEM((dil, lq + 2 * half, wcols), BF16),
                        pltpu.VMEM((dil, lq + 2 * half, wcols), BF16),
                        pltpu.VMEM((min(lq // qs, 3) * ATTN_HEADS, qs, qs + 2 * half), F32),
                        pltpu.VMEM((ATTN_HEADS, tokens, dh), F32),
                        pltpu.VMEM((tokens, LANES), F32),
                        pltpu.VMEM((tokens, LANES), F32)],
        compiler_params=_params(("parallel", "parallel", "arbitrary")),
        name="banded_attention",
    )(slopes, qkv, qkv, qkv, qkv, qkv, qkv, qkv)


def _combine_out_kernel(*refs, n_groups):
    acc_refs = refs[:n_groups]
    m_refs = refs[n_groups:2 * n_groups]
    l_refs = refs[2 * n_groups:3 * n_groups]
    x_ref, w_ref, out_ref, mix_ref = refs[3 * n_groups:]
    dh = x_ref.shape[1] // N_HEADS
    lses = [r[...] for r in l_refs]
    top = functools.reduce(jnp.maximum, lses)
    total = functools.reduce(lambda a, b: a + b, [jnp.exp2(l - top) for l in lses])
    lse_all = top + jnp.log2(total)
    coefs = [jnp.exp2(r[...] - lse_all) for r in m_refs]
    for h in range(N_HEADS):
        cols = slice(h * dh, (h + 1) * dh)
        mixed = functools.reduce(
            lambda a, b: a + b,
            [c[:, h:h + 1] * acc[:, cols].astype(F32) for c, acc in zip(coefs, acc_refs)])
        mix_ref[:, cols] = mixed.astype(mix_ref.dtype)
    out_ref[...] = x_ref[...] + jnp.dot(mix_ref[...], w_ref[...],
                                        preferred_element_type=F32)


def _combine_out(accs, maxes, lses, x2d, w_out, *, tm):
    T, D = x2d.shape
    n = len(accs)
    return pl.pallas_call(
        functools.partial(_combine_out_kernel, n_groups=n),
        grid=(T // tm,),
        in_specs=([pl.BlockSpec((tm, D), lambda i: (i, 0))] * n
                  + [pl.BlockSpec((tm, LANES), lambda i: (i, 0))] * (2 * n)
                  + [pl.BlockSpec((tm, D), lambda i: (i, 0)),
                     pl.BlockSpec((D, D), lambda i: (0, 0))]),
        out_specs=pl.BlockSpec((tm, D), lambda i: (i, 0)),
        out_shape=jax.ShapeDtypeStruct((T, D), F32),
        scratch_shapes=[pltpu.VMEM((tm, D), BF16)],
        compiler_params=_params(("parallel",)),
        name="combine_out",
    )(*accs, *maxes, *lses, x2d, w_out)


def kernel(x, mix_norm_g, ffn_norm_g, final_norm_g, sc_w_in, sc_conv_w, sc_conv_b, sc_w_out,
           attn_w_qkv, attn_w_out, ffn_w_up, ffn_conv_w, ffn_conv_b, ffn_w_down):
    B, S, D = x.shape
    T = B * S
    depth = mix_norm_g.shape[0]
    n_mixers = 2
    slopes = jnp.asarray(2.0 ** (-ALIBI_MAX * np.arange(1, N_HEADS + 1) / N_HEADS), dtype=F32)
    xf = x.reshape(T, D)

    def source(kind, layer):
        if kind == "mix":
            return (sc_w_in if layer % n_mixers == 0 else attn_w_qkv), layer // n_mixers
        if kind == "out":
            return (sc_w_out if layer % n_mixers == 0 else attn_w_out), layer // n_mixers
        return (ffn_w_up if kind == "up" else ffn_w_down), layer

    cast = {}

    def weight(kind, layer):
        return cast.get((kind, layer), source(kind, layer))

    def resident(kind, layer):
        w_all, idx = weight(kind, layer)
        return w_all[idx].astype(BF16)

    def run_with_side_casts(fn, wanted):
        todo = [key for key in wanted if key not in cast]
        out, copies = fn([source(*key) for key in todo])
        cast.update({key: (copy, 0) for key, copy in zip(todo, copies)})
        return out

    h_mix = None
    for i in range(depth):
        j = i // n_mixers
        w_mix, k_mix = weight("mix", i)
        if i % n_mixers == 0:
            y = run_with_side_casts(
                lambda side: _norm_in_gate(xf, mix_norm_g[i], w_mix, k_mix, sc_conv_w[j],
                                           sc_conv_b[j], side, seq_len=S, tm=1024, tn=512),
                [("out", i)])
            xf = run_with_side_casts(
                lambda side: _down_residual(y, resident("out", i), xf, None, None, side, tm=512),
                [("down", i)])
        else:
            parts = []
            for g, (window, dil) in enumerate(DILATED_GROUPS):
                qkv = _norm_matmul(xf if h_mix is None else h_mix, mix_norm_g[i], w_mix, k_mix,
                                   batch=B, dil=dil, col0=g * 3 * D, n_cols=3 * D, tm=1024,
                                   tn=2048 if w_mix.dtype == BF16 else 1024)
                parts.append(_banded_attention(qkv, slopes, window=window, dil=dil))
            accs, maxes, lses = zip(*parts)
            xf = _combine_out(accs, maxes, lses, xf, resident("out", i), tm=512)
        w_up, k_up = weight("up", i)
        later = [(kind, i + 1) for kind in ("mix", "up", "out", "down")] if i + 1 < depth else []
        g = run_with_side_casts(
            lambda side: _norm_up_gate(xf, ffn_norm_g[i], w_up, k_up, ffn_conv_w[i],
                                       ffn_conv_b[i], side, seq_len=S, tm=1024, tn=512),
            later)
        w_down = resident("down", i)
        h_mix = None
        if i == depth - 1:
            xf, _ = _down_residual(g, w_down, xf, final_norm_g, "final", [], tm=512)
        elif (i + 1) % n_mixers == 1:
            (xf, h_mix), _ = _down_residual(g, w_down, xf, mix_norm_g[i + 1], "next", [], tm=512)
        else:
            xf, _ = _down_residual(g, w_down, xf, None, None, [], tm=512)
    return xf.reshape(B, S, D)
```
